```python
import math
import jax, jax.numpy as jnp
from jax import lax
import numpy as np

D_MODEL = 1024
BATCH = 8
SEQ = 4096
DEPTH = 2

MEM_LEN = 256
EPS = 1e-6
D_FF = 2816
CONV_WIDTH = 3
CONV_CH = D_MODEL // 2
SWA_HEADS = 8
SWA_KV_HEADS = 2
SWA_HEAD_DIM = 64
WINDOW = 128
MOBA_HEADS = 8
MOBA_KV_HEADS = 4
MOBA_HEAD_DIM = D_MODEL // MOBA_HEADS
MOBA_BLOCK = 256
MOBA_TOPK = 3
MOBA_Q_CHUNK = 16
REL_BUCKETS = 32
REL_MAX_DIST = 128
REL_HEADS = SWA_HEADS
XA_HEADS = 4
XA_HEAD_DIM = 128
XA_W = XA_HEADS * XA_HEAD_DIM

N_EVEN = (DEPTH + 1) // 2
N_ODD = DEPTH // 2
SWA_Q_W = SWA_HEADS * SWA_HEAD_DIM
SWA_KV_W = SWA_KV_HEADS * SWA_HEAD_DIM
EVEN_IN = 3 * CONV_CH + SWA_Q_W + 2 * SWA_KV_W
EVEN_MIX = CONV_CH + SWA_Q_W
MOBA_Q_W = MOBA_HEADS * MOBA_HEAD_DIM
MOBA_KV_W = MOBA_KV_HEADS * MOBA_HEAD_DIM
ODD_IN = MOBA_Q_W + 2 * MOBA_KV_W

kernel_name = 'hybrid_shortconv_swa_moba_macaron'


def rmsnorm(x, g):
    x32 = x.astype(jnp.float32)
    y = x32 * lax.rsqrt(jnp.mean(x32 * x32, axis=-1, keepdims=True) + EPS)
    return y.astype(x.dtype) * g


def swiglu(x, w_gate, w_up, w_down):
    return (jax.nn.silu(x @ w_gate) * (x @ w_up)) @ w_down


def rel_bucket(dist):
    n = jnp.maximum(dist, 0)
    max_exact = REL_BUCKETS // 2
    nf = jnp.maximum(n, 1).astype(jnp.float32)
    large = max_exact + (jnp.log(nf / max_exact) / math.log(REL_MAX_DIST / max_exact)
                         * (REL_BUCKETS - max_exact)).astype(jnp.int32)
    large = jnp.minimum(large, REL_BUCKETS - 1)
    return jnp.where(n < max_exact, n, large)


def short_conv_mixer(b_gate, c_gate, u, conv_w):
    v = c_gate * u
    S = v.shape[1]
    vp = jnp.pad(v, ((0, 0), (CONV_WIDTH - 1, 0), (0, 0)))
    conv = vp[:, 0:S] * conv_w[0]
    for j in range(1, CONV_WIDTH):
        conv = conv + vp[:, j:j + S] * conv_w[j]
    return b_gate * conv


def sliding_window_attention(q, k, v, sinks, table):
    B, S, H, dh = q.shape
    Hkv = k.shape[2]
    G = H // Hkv
    nb = S // WINDOW
    qb = q.reshape(B, nb, WINDOW, Hkv, G, dh)

    def with_prev(t):
        t = t.reshape(B, nb, WINDOW, Hkv, dh)
        prev = jnp.pad(t, ((0, 0), (1, 0), (0, 0), (0, 0), (0, 0)))[:, :-1]
        return jnp.concatenate([prev, t], axis=2)

    kc, vc = with_prev(k), with_prev(v)
    logits = jnp.einsum('bnqkgd,bnjkd->bnkgqj', qb, kc).astype(jnp.float32) * dh ** -0.5
    qi = jnp.arange(WINDOW)[:, None]
    kj = jnp.arange(2 * WINDOW)[None, :]
    dist = qi + WINDOW - kj
    band = (dist >= 0) & (dist < WINDOW)
    mask = band[None] & ((jnp.arange(nb)[:, None, None] > 0) | (kj >= WINDOW)[None])
    bias = jnp.moveaxis(table[rel_bucket(dist)], -1, 0).reshape(Hkv, G, WINDOW, 2 * WINDOW)
    logits = jnp.where(mask[None, :, None, None], logits + bias, -jnp.inf)
    sink = sinks.astype(jnp.float32).reshape(Hkv, G)[None, None, :, :, None, None]
    m = jnp.maximum(jnp.max(logits, axis=-1, keepdims=True), sink)
    p = jnp.exp(logits - m)
    p = p / (jnp.sum(p, axis=-1, keepdims=True) + jnp.exp(sink - m))
    out = jnp.einsum('bnkgqj,bnjkd->bnqkgd', p.astype(v.dtype), vc)
    return out.reshape(B, S, H * dh)


def moba_attention(q, k, v, table):
    B, S, H, dh = q.shape
    Hkv = k.shape[2]
    G = H // Hkv
    s_pad = -(-S // MOBA_BLOCK) * MOBA_BLOCK
    nblk = s_pad // MOBA_BLOCK
    pad = ((0, 0), (0, s_pad - S), (0, 0), (0, 0))
    kblk = jnp.pad(k, pad).reshape(B, nblk, MOBA_BLOCK, Hkv, dh).transpose(0, 3, 1, 2, 4)
    vblk = jnp.pad(v, pad).reshape(B, nblk, MOBA_BLOCK, Hkv, dh).transpose(0, 3, 1, 2, 4)
    kmean = jnp.mean(kblk.astype(jnp.float32), axis=3)
    nq = S // MOBA_Q_CHUNK
    qc = q.reshape(B, nq, MOBA_Q_CHUNK, Hkv, G, dh).transpose(1, 0, 3, 4, 2, 5)
    topk = min(MOBA_TOPK, nblk)
    n_sel = topk * MOBA_BLOCK
    scale = dh ** -0.5
    b_ix = jnp.arange(B)[:, None, None, None, None]
    k_ix = jnp.arange(Hkv)[None, :, None, None, None]
    tbl = table.T.reshape(Hkv, G, REL_BUCKETS)
    k6 = jnp.arange(Hkv)[None, :, None, None, None, None]
    g6 = jnp.arange(G)[None, None, :, None, None, None]
    blk_ids = jnp.arange(nblk)
    offs = jnp.arange(MOBA_BLOCK)

    def one_chunk(args):
        ci, qx = args
        t = ci * MOBA_Q_CHUNK + jnp.arange(MOBA_Q_CHUNK)
        own = (ci * MOBA_Q_CHUNK) // MOBA_BLOCK
        gate = jnp.einsum('bkgqd,bknd->bkgqn', qx.astype(jnp.float32), kmean)
        gate = jnp.where(blk_ids < own, gate, -jnp.inf)
        _, sel = lax.top_k(gate, topk)
        valid = sel < own
        ks = kblk[b_ix, k_ix, sel]
        vs = vblk[b_ix, k_ix, sel]
        dist_sel = t[:, None, None] - (sel[..., None] * MOBA_BLOCK + offs)
        l_sel = jnp.einsum('bkgqd,bkgqsnd->bkgqsn', qx, ks).astype(jnp.float32) * scale
        l_sel = jnp.where(valid[..., None], l_sel + tbl[k6, g6, rel_bucket(dist_sel)], -jnp.inf)
        k_own = lax.dynamic_index_in_dim(kblk, own, axis=2, keepdims=False)
        v_own = lax.dynamic_index_in_dim(vblk, own, axis=2, keepdims=False)
        dist_own = t[:, None] - (own * MOBA_BLOCK + offs)[None, :]
        l_own = jnp.einsum('bkgqd,bknd->bkgqn', qx, k_own).astype(jnp.float32) * scale
        l_own = jnp.where(dist_own >= 0, l_own + tbl[:, :, rel_bucket(dist_own)], -jnp.inf)
        logits = jnp.concatenate([l_sel.reshape(l_sel.shape[:4] + (n_sel,)), l_own], axis=-1)
        p = jax.nn.softmax(logits, axis=-1).astype(v.dtype)
        p_sel = p[..., :n_sel].reshape(l_sel.shape)
        return (jnp.einsum('bkgqsn,bkgqsnd->bkgqd', p_sel, vs)
                + jnp.einsum('bkgqn,bknd->bkgqd', p[..., n_sel:], v_own))

    out = lax.map(one_chunk, (jnp.arange(nq), qc))
    return out.transpose(1, 0, 4, 2, 3, 5).reshape(B, S, H * dh)


def even_mixer(h, w_in, conv_w, sinks, w_out, table):
    B, S, _ = h.shape
    z = h @ w_in
    c1 = CONV_CH
    c2 = 2 * CONV_CH
    c3 = 3 * CONV_CH
    c4 = c3 + SWA_Q_W
    c5 = c4 + SWA_KV_W
    b_gate, c_gate, u, q, k, v = jnp.split(z, [c1, c2, c3, c4, c5], axis=-1)
    ya = short_conv_mixer(b_gate, c_gate, u, conv_w)
    yb = sliding_window_attention(q.reshape(B, S, SWA_HEADS, SWA_HEAD_DIM),
                                  k.reshape(B, S, SWA_KV_HEADS, SWA_HEAD_DIM),
                                  v.reshape(B, S, SWA_KV_HEADS, SWA_HEAD_DIM), sinks, table)
    return jnp.concatenate([ya, yb], axis=-1) @ w_out


def odd_mixer(h, w_in, w_out, table):
    B, S, _ = h.shape
    z = h @ w_in
    q, k, v = jnp.split(z, [MOBA_Q_W, MOBA_Q_W + MOBA_KV_W], axis=-1)
    y = moba_attention(q.reshape(B, S, MOBA_HEADS, MOBA_HEAD_DIM),
                       k.reshape(B, S, MOBA_KV_HEADS, MOBA_HEAD_DIM),
                       v.reshape(B, S, MOBA_KV_HEADS, MOBA_HEAD_DIM), table)
    return y @ w_out


def memory_cross_attention(h, mem_n, w_q, w_kv, w_o):
    B, S, _ = h.shape
    M = mem_n.shape[1]
    q = (h @ w_q).reshape(B, S, XA_HEADS, XA_HEAD_DIM)
    kv = (mem_n @ w_kv).reshape(B, M, 2, XA_HEADS, XA_HEAD_DIM)
    k, v = kv[:, :, 0], kv[:, :, 1]
    logits = jnp.einsum('bshd,bmhd->bhsm', q, k).astype(jnp.float32) * XA_HEAD_DIM ** -0.5
    p = jax.nn.softmax(logits, axis=-1).astype(v.dtype)
    o = jnp.einsum('bhsm,bmhd->bshd', p, v).reshape(B, S, XA_W)
    return o @ w_o


def setup_inputs(seed: int = 0) -> dict:
    key = jax.random.key(seed)
    ks = jax.random.split(key, 32)
    f32 = jnp.float32

    def w(k, shape, fan_in):
        return jax.random.normal(k, shape, f32) * fan_in ** -0.5

    def gain(k, shape):
        return 1.0 + 0.02 * jax.random.normal(k, shape, f32)

    return {
        'x': jax.random.normal(ks[0], (BATCH, SEQ, D_MODEL), f32),
        'mem': jax.random.normal(ks[1], (BATCH, MEM_LEN, D_MODEL), f32),
        'ffn1_norm': gain(ks[2], (DEPTH, D_MODEL)),
        'ffn1_w_gate': w(ks[3], (DEPTH, D_MODEL, D_FF), D_MODEL),
        'ffn1_w_up': w(ks[4], (DEPTH, D_MODEL, D_FF), D_MODEL),
        'ffn1_w_down': w(ks[5], (DEPTH, D_FF, D_MODEL), D_FF),
        'mix_norm': gain(ks[6], (DEPTH, D_MODEL)),
        'ev_w_in': w(ks[7], (N_EVEN, D_MODEL, EVEN_IN), D_MODEL),
        'ev_conv_w': w(ks[8], (N_EVEN, CONV_WIDTH, CONV_CH), CONV_WIDTH),
        'ev_sinks': 0.5 * jax.random.normal(ks[9], (N_EVEN, SWA_HEADS), f32),
        'ev_w_out': w(ks[10], (N_EVEN, EVEN_MIX, D_MODEL), EVEN_MIX),
        'od_w_in': w(ks[11], (N_ODD, D_MODEL, ODD_IN), D_MODEL),
        'od_w_out': w(ks[12], (N_ODD, MOBA_Q_W, D_MODEL), MOBA_Q_W),
        'rel_bias': 0.2 * jax.random.normal(ks[13], (REL_BUCKETS, REL_HEADS), f32),
        'xa_norm': gain(ks[14], (DEPTH, D_MODEL)),
        'xa_w_q': w(ks[15], (DEPTH, D_MODEL, XA_W), D_MODEL),
        'xa_w_kv': w(ks[16], (DEPTH, D_MODEL, 2 * XA_W), D_MODEL),
        'xa_w_o': w(ks[17], (DEPTH, XA_W, D_MODEL), XA_W),
        'mem_norm': gain(ks[18], (D_MODEL,)),
        'ffn2_norm': gain(ks[19], (DEPTH, D_MODEL)),
        'ffn2_w_gate': w(ks[20], (DEPTH, D_MODEL, D_FF), D_MODEL),
        'ffn2_w_up': w(ks[21], (DEPTH, D_MODEL, D_FF), D_MODEL),
        'ffn2_w_down': w(ks[22], (DEPTH, D_FF, D_MODEL), D_FF),
        'final_norm': gain(ks[23], (D_MODEL,)),
    }


def reference(x, mem, ffn1_norm, ffn1_w_gate, ffn1_w_up, ffn1_w_down, mix_norm,
              ev_w_in, ev_conv_w, ev_sinks, ev_w_out, od_w_in, od_w_out, rel_bias,
              xa_norm, xa_w_q, xa_w_kv, xa_w_o, mem_norm,
              ffn2_norm, ffn2_w_gate, ffn2_w_up, ffn2_w_down, final_norm):
    mem_n = rmsnorm(mem, mem_norm)
    for l in range(DEPTH):
        x = x + 0.5 * swiglu(rmsnorm(x, ffn1_norm[l]), ffn1_w_gate[l], ffn1_w_up[l], ffn1_w_down[l])
        h = rmsnorm(x, mix_norm[l])
        if l % 2 == 0:
            i = l // 2
            x = x + even_mixer(h, ev_w_in[i], ev_conv_w[i], ev_sinks[i], ev_w_out[i], rel_bias)
        else:
            i = l // 2
            x = x + odd_mixer(h, od_w_in[i], od_w_out[i], rel_bias)
        x = x + memory_cross_attention(rmsnorm(x, xa_norm[l]), mem_n, xa_w_q[l], xa_w_kv[l], xa_w_o[l])
        x = x + 0.5 * swiglu(rmsnorm(x, ffn2_norm[l]), ffn2_w_gate[l], ffn2_w_up[l], ffn2_w_down[l])
    return rmsnorm(x, final_norm)
```

```python
import functools
import math

import jax
import jax.numpy as jnp
import numpy as np
from jax import lax
from jax.experimental import pallas as pl
from jax.experimental.pallas import tpu as pltpu

EPS = 1e-6
CONV_WIDTH = 3
CONV_CH = 512
SWA_HEADS = 8
SWA_KV_HEADS = 2
SWA_HEAD_DIM = 64
WINDOW = 128
MOBA_HEADS = 8
MOBA_KV_HEADS = 4
MOBA_HEAD_DIM = 128
MOBA_BLOCK = 256
MOBA_TOPK = 3
REL_BUCKETS = 32
REL_MAX_DIST = 128
XA_HEADS = 4
XA_HEAD_DIM = 128

V7X_VMEM_BYTES = 64 * 1024 * 1024
VMEM_LIMIT_BYTES = V7X_VMEM_BYTES * 3 // 4
MASKED = -1e30

_NT = (((1,), (1,)), ((), ()))


def _params(*semantics):
    return pltpu.CompilerParams(dimension_semantics=semantics, vmem_limit_bytes=VMEM_LIMIT_BYTES)


def _resident(shape):
    zeros = (0,) * len(shape)
    return pl.BlockSpec(shape, lambda *_: zeros, pipeline_mode=pl.Buffered(1))


def _rms(x, g):
    return x * lax.rsqrt(jnp.mean(x * x, axis=-1, keepdims=True) + EPS) * g


def _dot(a, b):
    return jnp.dot(a, b, preferred_element_type=jnp.float32)


def _dot_nt(a, b):
    return lax.dot_general(a, b, _NT, preferred_element_type=jnp.float32)


def _ffn_kernel(x_ref, g_ref, wg_ref, wu_ref, wd_ref, fg_ref, o_ref, *, ff_chunk, final_norm):
    x = x_ref[...]
    n = _rms(x, g_ref[...]).astype(jnp.bfloat16)
    d_ff = wg_ref.shape[1]
    acc = jnp.zeros(x.shape, jnp.float32)
    for c in range(d_ff // ff_chunk):
        cols = slice(c * ff_chunk, (c + 1) * ff_chunk)
        gate = _dot(n, wg_ref[:, cols])
        up = _dot(n, wu_ref[:, cols])
        h = (gate * jax.nn.sigmoid(gate) * up).astype(jnp.bfloat16)
        acc = acc + _dot(h, wd_ref[cols, :])
    y = x + 0.5 * acc
    if final_norm:
        y = _rms(y, fg_ref[...])
    o_ref[...] = y


def _ffn(x, g, wg, wu, wd, final_g, *, final_norm, tm=512, ff_chunk=256):
    t, d = x.shape
    d_ff = wg.shape[1]
    assert t % tm == 0 and d_ff % ff_chunk == 0
    row = pl.BlockSpec((tm, d), lambda i: (i, 0))
    return pl.pallas_call(
        functools.partial(_ffn_kernel, ff_chunk=ff_chunk, final_norm=final_norm),
        grid=(t // tm,),
        in_specs=[row, _resident((1, d)), _resident((d, d_ff)), _resident((d, d_ff)),
                  _resident((d_ff, d)), _resident((1, d))],
        out_specs=row,
        out_shape=jax.ShapeDtypeStruct((t, d), jnp.float32),
        compiler_params=_params("parallel"),
        name="ffn",
    )(x, g, wg, wu, wd, final_g)


def _norm_proj_kernel(x_ref, g_ref, w_ref, o_ref):
    n = _rms(x_ref[...], g_ref[...]).astype(jnp.bfloat16)
    o_ref[...] = _dot(n, w_ref[...]).astype(o_ref.dtype)


def _norm_proj(x, g, w, *, tm=512):
    t, d = x.shape
    n_out = w.shape[1]
    assert t % tm == 0
    return pl.pallas_call(
        _norm_proj_kernel,
        grid=(t // tm,),
        in_specs=[pl.BlockSpec((tm, d), lambda i: (i, 0)), _resident((1, d)), _resident((d, n_out))],
        out_specs=pl.BlockSpec((tm, n_out), lambda i: (i, 0)),
        out_shape=jax.ShapeDtypeStruct((t, n_out), jnp.bfloat16),
        compiler_params=_params("parallel"),
        name="norm_proj",
    )(x, g, w)


def _proj_residual_kernel(x_ref, y_ref, w_ref, o_ref):
    o_ref[...] = x_ref[...] + _dot(y_ref[...], w_ref[...])


def _proj_residual(x, y, w, *, tm=512):
    t, d = x.shape
    k = y.shape[1]
    assert t % tm == 0
    return pl.pallas_call(
        _proj_residual_kernel,
        grid=(t // tm,),
        in_specs=[pl.BlockSpec((tm, d), lambda i: (i, 0)), pl.BlockSpec((tm, k), lambda i: (i, 0)),
                  _resident((k, d))],
        out_specs=pl.BlockSpec((tm, d), lambda i: (i, 0)),
        out_shape=jax.ShapeDtypeStruct((t, d), jnp.float32),
        compiler_params=_params("parallel"),
        name="proj_residual",
    )(x, y, w)


def _rel_bucket(dist):
    n = np.maximum(dist, 0)
    max_exact = REL_BUCKETS // 2
    nf = np.maximum(n, 1).astype(np.float32)
    scaled = (np.log(nf / np.float32(max_exact)) / np.float32(math.log(REL_MAX_DIST / max_exact))
              * np.float32(REL_BUCKETS - max_exact))
    large = np.minimum(max_exact + scaled.astype(np.int32), REL_BUCKETS - 1)
    return np.where(n < max_exact, n, large).astype(np.int32)


def _bias_tile(table, dist, visible, kv_heads):
    q, k = dist.shape
    heads = table.shape[1]
    bias = jnp.where(visible[:, :, None], table[_rel_bucket(dist)], MASKED)
    return jnp.moveaxis(bias, -1, 0).reshape(kv_heads, (heads // kv_heads) * q, k)


def _even_mixer_kernel(sinks_ref, x_ref, z_ref, kvh_ref, ch_ref, uh_ref, convw_ref, bias_ref,
                       wout_ref, o_ref, y_ref, *, halo_rows):
    i = pl.program_id(1)
    ts = z_ref.shape[1]
    c1, c2, c3 = CONV_CH, 2 * CONV_CH, 3 * CONV_CH
    q0 = c3
    k0 = q0 + SWA_HEADS * SWA_HEAD_DIM
    v0 = k0 + SWA_KV_HEADS * SWA_HEAD_DIM
    group = SWA_HEADS // SWA_KV_HEADS
    dh = SWA_HEAD_DIM
    first = i == 0

    f32 = jnp.float32
    v = z_ref[0, :, c1:c2].astype(f32) * z_ref[0, :, c2:c3].astype(f32)
    vh = ch_ref[0].astype(f32) * uh_ref[0].astype(f32)
    vh = jnp.where(first, 0.0, vh)
    row = lax.broadcasted_iota(jnp.int32, v.shape, 0)
    h1 = vh[halo_rows - 1:halo_rows, :]
    h2 = vh[halo_rows - 2:halo_rows - 1, :]
    v1 = jnp.where(row == 0, h1, pltpu.roll(v, 1, 0))
    v2 = jnp.where(row == 0, h2, jnp.where(row == 1, h1, pltpu.roll(v, 2, 0)))
    conv = v2 * convw_ref[0:1, :] + v1 * convw_ref[1:2, :] + v * convw_ref[2:3, :]
    y_ref[:, 0:CONV_CH] = (z_ref[0, :, 0:c1].astype(f32) * conv).astype(y_ref.dtype)

    scale = dh ** -0.5
    col = lax.broadcasted_iota(jnp.int32, (WINDOW, 2 * WINDOW), 1)
    for qb in range(ts // WINDOW):
        rows = slice(qb * WINDOW, (qb + 1) * WINDOW)
        for kh in range(SWA_KV_HEADS):
            kcol = slice(k0 + kh * dh, k0 + (kh + 1) * dh)
            vcol = slice(v0 + kh * dh, v0 + (kh + 1) * dh)
            if qb == 0:
                k_prev = kvh_ref[0, :, kh * dh:(kh + 1) * dh]
                v_prev = kvh_ref[0, :, (SWA_KV_HEADS + kh) * dh:(SWA_KV_HEADS + kh + 1) * dh]
            else:
                prev_rows = slice((qb - 1) * WINDOW, qb * WINDOW)
                k_prev = z_ref[0, prev_rows, kcol]
                v_prev = z_ref[0, prev_rows, vcol]
            kc = jnp.concatenate([k_prev, z_ref[0, rows, kcol]], axis=0)
            vc = jnp.concatenate([v_prev, z_ref[0, rows, vcol]], axis=0)
            qs = jnp.concatenate(
                [z_ref[0, rows, q0 + (kh * group + g) * dh:q0 + (kh * group + g + 1) * dh]
                 for g in range(group)], axis=0)
            logits = _dot_nt(qs, kc) * scale + bias_ref[kh]
            for g in range(group):
                head = kh * group + g
                lg = logits[g * WINDOW:(g + 1) * WINDOW]
                if qb == 0:
                    lg = jnp.where(first & (col < WINDOW), MASKED, lg)
                sink = sinks_ref[head]
                m = jnp.maximum(jnp.max(lg, axis=-1, keepdims=True), sink)
                p = jnp.exp(lg - m)
                denom = jnp.sum(p, axis=-1, keepdims=True) + jnp.exp(sink - m)
                out = _dot(p.astype(jnp.bfloat16), vc) / denom
                y_ref[rows, CONV_CH + head * dh:CONV_CH + (head + 1) * dh] = out.astype(y_ref.dtype)

    o_ref[0] = x_ref[0] + _dot(y_ref[...], wout_ref[...])


def _even_mixer(x, z, conv_w, sinks, w_out, table, *, ts=512, halo_rows=16):
    b, s, d = x.shape
    n_in = z.shape[-1]
    assert s % ts == 0 and ts % WINDOW == 0
    dist = np.arange(WINDOW)[:, None] + WINDOW - np.arange(2 * WINDOW)[None, :]
    bias = _bias_tile(table, dist, (dist >= 0) & (dist < WINDOW), SWA_KV_HEADS)
    kv_w = 2 * SWA_KV_HEADS * SWA_HEAD_DIM
    kv_block = (3 * CONV_CH + SWA_HEADS * SWA_HEAD_DIM) // kv_w
    w_per_tile = ts // WINDOW
    h_per_tile = ts // halo_rows
    return pl.pallas_call(
        functools.partial(_even_mixer_kernel, halo_rows=halo_rows),
        grid=(b, s // ts),
        in_specs=[
            pl.BlockSpec(memory_space=pltpu.SMEM),
            pl.BlockSpec((1, ts, d), lambda bi, i: (bi, i, 0)),
            pl.BlockSpec((1, ts, n_in), lambda bi, i: (bi, i, 0)),
            pl.BlockSpec((1, WINDOW, kv_w), lambda bi, i: (bi, jnp.maximum(i * w_per_tile - 1, 0), kv_block)),
            pl.BlockSpec((1, halo_rows, CONV_CH), lambda bi, i: (bi, jnp.maximum(i * h_per_tile - 1, 0), 1)),
            pl.BlockSpec((1, halo_rows, CONV_CH), lambda bi, i: (bi, jnp.maximum(i * h_per_tile - 1, 0), 2)),
            _resident(conv_w.shape),
            _resident(bias.shape),
            _resident(w_out.shape),
        ],
        out_specs=pl.BlockSpec((1, ts, d), lambda bi, i: (bi, i, 0)),
        out_shape=jax.ShapeDtypeStruct((b, s, d), jnp.float32),
        scratch_shapes=[pltpu.VMEM((ts, w_out.shape[0]), jnp.bfloat16)],
        compiler_params=_params("parallel", "parallel"),
        name="even_mixer",
    )(sinks, x, z, z, z, z, conv_w, bias, w_out)


def _moba_kernel(far_ref, q_ref, k_ref, v_ref, bown_ref, bprev_ref, o_ref, kmean_ref):
    i = pl.program_id(2)
    blk = MOBA_BLOCK
    dh = MOBA_HEAD_DIM
    group = q_ref.shape[2] // dh
    nblk = k_ref.shape[1] // blk
    rows = group * blk
    scale = dh ** -0.5
    f32 = jnp.float32

    @pl.when(i == 0)
    def _():
        for j in range(nblk):
            kj = k_ref[0, j * blk:(j + 1) * blk, :].astype(f32)
            kmean_ref[j:j + 1, :] = jnp.mean(kj, axis=0, keepdims=True)

    q = q_ref[0]
    qs = jnp.concatenate([q[:, g * dh:(g + 1) * dh] for g in range(group)], axis=0)
    gate = lax.dot_general(qs.astype(f32), kmean_ref[...], _NT, precision=lax.Precision.HIGHEST,
                           preferred_element_type=f32)
    lane = lax.broadcasted_iota(jnp.int32, gate.shape, 1)
    past = lane < i

    def selected(j):
        gj = jnp.sum(jnp.where(lane == j, gate, 0.0), axis=-1, keepdims=True)
        beats = ((gate > gj) | ((gate == gj) & (lane < j))) & past
        return jnp.sum(beats.astype(f32), axis=-1, keepdims=True) < MOBA_TOPK

    def block_logits(j):
        start = pl.multiple_of(j * blk, blk)
        return _dot_nt(qs, k_ref[0, pl.ds(start, blk), :]) * scale

    def fold(j, s, m, l, acc):
        start = pl.multiple_of(j * blk, blk)
        m_new = jnp.maximum(m, jnp.max(s, axis=-1, keepdims=True))
        alpha = jnp.exp(m - m_new)
        p = jnp.exp(s - m_new)
        l = alpha * l + jnp.sum(p, axis=-1, keepdims=True)
        acc = alpha * acc + _dot(p.astype(jnp.bfloat16), v_ref[0, pl.ds(start, blk), :])
        return m_new, l, acc

    s = block_logits(i) + bown_ref[0]
    m = jnp.max(s, axis=-1, keepdims=True)
    p = jnp.exp(s - m)
    l = jnp.sum(p, axis=-1, keepdims=True)
    start_own = pl.multiple_of(i * blk, blk)
    acc = _dot(p.astype(jnp.bfloat16), v_ref[0, pl.ds(start_own, blk), :])

    def prev_block(carry):
        m, l, acc = carry
        j = i - 1
        s = jnp.where(selected(j), block_logits(j) + bprev_ref[0], MASKED)
        return fold(j, s, m, l, acc)

    m, l, acc = lax.cond(i >= 1, prev_block, lambda c: c, (m, l, acc))

    row = lax.broadcasted_iota(jnp.int32, (rows, 1), 0)
    hk = pl.program_id(1)
    far = jnp.zeros((rows, 1), f32)
    for g in range(group):
        far = jnp.where(row // blk == g, far_ref[hk * group + g], far)

    def far_block(j, carry):
        m, l, acc = carry
        s = jnp.where(selected(j), block_logits(j) + far, MASKED)
        return fold(j, s, m, l, acc)

    m, l, acc = lax.fori_loop(0, jnp.maximum(i - 1, 0), far_block, (m, l, acc))

    out = (acc / l).astype(o_ref.dtype)
    for g in range(group):
        o_ref[0, :, g * dh:(g + 1) * dh] = out[g * blk:(g + 1) * blk]


def _moba(z, table):
    b, s, _ = z.shape
    blk, dh = MOBA_BLOCK, MOBA_HEAD_DIM
    assert s % blk == 0
    nblk = s // blk
    group = MOBA_HEADS // MOBA_KV_HEADS
    qi = np.arange(blk)[:, None]
    kj = np.arange(blk)[None, :]
    bias_own = _bias_tile(table, qi - kj, qi - kj >= 0, MOBA_KV_HEADS)
    bias_prev = _bias_tile(table, qi + blk - kj, np.ones((blk, blk), bool), MOBA_KV_HEADS)
    far = table[REL_BUCKETS - 1]
    assert _rel_bucket(np.array([blk + 1]))[0] == REL_BUCKETS - 1
    k_block0 = MOBA_HEADS
    v_block0 = MOBA_HEADS + MOBA_KV_HEADS
    return pl.pallas_call(
        _moba_kernel,
        grid=(b, MOBA_KV_HEADS, nblk),
        in_specs=[
            pl.BlockSpec(memory_space=pltpu.SMEM),
            pl.BlockSpec((1, blk, group * dh), lambda bi, h, i: (bi, i, h)),
            pl.BlockSpec((1, s, dh), lambda bi, h, i: (bi, 0, k_block0 + h)),
            pl.BlockSpec((1, s, dh), lambda bi, h, i: (bi, 0, v_block0 + h)),
            pl.BlockSpec((1, group * blk, blk), lambda bi, h, i: (h, 0, 0)),
            pl.BlockSpec((1, group * blk, blk), lambda bi, h, i: (h, 0, 0)),
        ],
        out_specs=pl.BlockSpec((1, blk, group * dh), lambda bi, h, i: (bi, i, h)),
        out_shape=jax.ShapeDtypeStruct((b, s, MOBA_HEADS * dh), jnp.bfloat16),
        scratch_shapes=[pltpu.VMEM((nblk, dh), jnp.float32)],
        compiler_params=_params("parallel", "parallel", "arbitrary"),
        name="moba",
    )(far, z, z, z, bias_own, bias_prev)


def _xattn_kernel(x_ref, g_ref, wq_ref, kv_ref, wo_ref, o_ref, a_ref):
    dh = XA_HEAD_DIM
    x = x_ref[...]
    n = _rms(x, g_ref[...]).astype(jnp.bfloat16)
    q = _dot(n, wq_ref[...]).astype(jnp.bfloat16)
    scale = dh ** -0.5
    v0 = XA_HEADS * dh
    for h in range(XA_HEADS):
        cols = slice(h * dh, (h + 1) * dh)
        s = _dot_nt(q[:, cols], kv_ref[:, cols]) * scale
        m = jnp.max(s, axis=-1, keepdims=True)
        p = jnp.exp(s - m)
        l = jnp.sum(p, axis=-1, keepdims=True)
        out = _dot(p.astype(jnp.bfloat16), kv_ref[:, v0 + h * dh:v0 + (h + 1) * dh]) / l
        a_ref[:, cols] = out.astype(a_ref.dtype)
    o_ref[...] = x + _dot(a_ref[...], wo_ref[...])


def _xattn(x, g, w_q, kv, layer, w_o, *, seq, mem_len, tm=512):
    t, d = x.shape
    xa_w = XA_HEADS * XA_HEAD_DIM
    assert seq % tm == 0
    tiles_per_seq = seq // tm
    return pl.pallas_call(
        _xattn_kernel,
        grid=(t // tm,),
        in_specs=[
            pl.BlockSpec((tm, d), lambda i: (i, 0)),
            _resident((1, d)),
            _resident((d, xa_w)),
            pl.BlockSpec((mem_len, 2 * xa_w), lambda i: (i // tiles_per_seq, layer)),
            _resident((xa_w, d)),
        ],
        out_specs=pl.BlockSpec((tm, d), lambda i: (i, 0)),
        out_shape=jax.ShapeDtypeStruct((t, d), jnp.float32),
        scratch_shapes=[pltpu.VMEM((tm, xa_w), jnp.bfloat16)],
        compiler_params=_params("parallel"),
        name="xattn",
    )(x, g, w_q, kv, w_o)


def kernel(x, mem, ffn1_norm, ffn1_w_gate, ffn1_w_up, ffn1_w_down, mix_norm, ev_w_in, ev_conv_w,
           ev_sinks, ev_w_out, od_w_in, od_w_out, rel_bias, xa_norm, xa_w_q, xa_w_kv, xa_w_o,
           mem_norm, ffn2_norm, ffn2_w_gate, ffn2_w_up, ffn2_w_down, final_norm):
    b, s, d = x.shape
    depth = ffn1_norm.shape[0]
    mem_len = mem.shape[1]
    bf16 = jnp.bfloat16
    row = lambda g: g.reshape(1, d)

    kv_w = jnp.concatenate([xa_w_kv[l] for l in range(depth)], axis=1).astype(bf16)
    kv = _norm_proj(mem.reshape(b * mem_len, d), row(mem_norm), kv_w, tm=min(512, b * mem_len))

    xt = x.reshape(b * s, d)
    for l in range(depth):
        xt = _ffn(xt, row(ffn1_norm[l]), ffn1_w_gate[l].astype(bf16), ffn1_w_up[l].astype(bf16),
                  ffn1_w_down[l].astype(bf16), row(final_norm), final_norm=False)
        i = l // 2
        if l % 2 == 0:
            z = _norm_proj(xt, row(mix_norm[l]), ev_w_in[i].astype(bf16))
            xt = _even_mixer(xt.reshape(b, s, d), z.reshape(b, s, -1), ev_conv_w[i], ev_sinks[i],
                             ev_w_out[i].astype(bf16), rel_bias).reshape(b * s, d)
        else:
            z = _norm_proj(xt, row(mix_norm[l]), od_w_in[i].astype(bf16))
            y = _moba(z.reshape(b, s, -1), rel_bias)
            xt = _proj_residual(xt, y.reshape(b * s, -1), od_w_out[i].astype(bf16))
        xt = _xattn(xt, row(xa_norm[l]), xa_w_q[l].astype(bf16), kv, l, xa_w_o[l].astype(bf16),
                    seq=s, mem_len=mem_len)
        xt = _ffn(xt, row(ffn2_norm[l]), ffn2_w_gate[l].astype(bf16), ffn2_w_up[l].astype(bf16),
                  ffn2_w_down[l].astype(bf16), row(final_norm), final_norm=(l == depth - 1))
    return xt.reshape(b, s, d)
```

```python
import functools
import math

import jax
import jax.numpy as jnp
import numpy as np
from jax import lax
from jax.experimental import pallas as pl
from jax.experimental.pallas import tpu as pltpu

EPS = 1e-6
CONV_WIDTH = 3
CONV_CH = 512
SWA_HEADS = 8
SWA_KV_HEADS = 2
SWA_HEAD_DIM = 64
WINDOW = 128
MOBA_HEADS = 8
MOBA_KV_HEADS = 4
MOBA_HEAD_DIM = 128
MOBA_BLOCK = 256
MOBA_TOPK = 3
REL_BUCKETS = 32
REL_MAX_DIST = 128
XA_HEADS = 4
XA_HEAD_DIM = 128

V7X_VMEM_BYTES = 64 * 1024 * 1024
VMEM_LIMIT_BYTES = V7X_VMEM_BYTES * 3 // 4
V7X_LANES = 128
MASKED = -1e30
LOG2E = math.log2(math.e)

_NT = (((1,), (1,)), ((), ()))


def _params(*semantics):
    return pltpu.CompilerParams(dimension_semantics=semantics, vmem_limit_bytes=VMEM_LIMIT_BYTES)


def _resident(shape):
    zeros = (0,) * len(shape)
    return pl.BlockSpec(shape, lambda *_: zeros, pipeline_mode=pl.Buffered(1))


def _rms(x, g):
    return x * lax.rsqrt(jnp.mean(x * x, axis=-1, keepdims=True) + EPS) * g


def _dot(a, b):
    return jnp.dot(a, b, preferred_element_type=jnp.float32)


def _dot_nt(a, b):
    return lax.dot_general(a, b, _NT, preferred_element_type=jnp.float32)


def _ffn_kernel(x_ref, g_ref, wg_ref, wu_ref, wd_ref, fg_ref, o_ref, *, ff_chunk, final_norm):
    x = x_ref[...]
    n = _rms(x, g_ref[...]).astype(jnp.bfloat16)
    d_ff = wg_ref.shape[1]
    acc = jnp.zeros(x.shape, jnp.float32)
    for c in range(d_ff // ff_chunk):
        cols = slice(c * ff_chunk, (c + 1) * ff_chunk)
        gate = _dot(n, wg_ref[:, cols])
        up = _dot(n, wu_ref[:, cols])
        h = (gate * jax.nn.sigmoid(gate) * up).astype(jnp.bfloat16)
        acc = acc + _dot(h, wd_ref[cols, :])
    y = x + 0.5 * acc
    if final_norm:
        y = _rms(y, fg_ref[...])
    o_ref[...] = y


def _ffn(x, g, wg, wu, wd, final_g, *, final_norm, tm=512, ff_chunk=256):
    t, d = x.shape
    d_ff = wg.shape[1]
    assert t % tm == 0 and d_ff % ff_chunk == 0
    row = pl.BlockSpec((tm, d), lambda i: (i, 0))
    return pl.pallas_call(
        functools.partial(_ffn_kernel, ff_chunk=ff_chunk, final_norm=final_norm),
        grid=(t // tm,),
        in_specs=[row, _resident((1, d)), _resident((d, d_ff)), _resident((d, d_ff)),
                  _resident((d_ff, d)), _resident((1, d))],
        out_specs=row,
        out_shape=jax.ShapeDtypeStruct((t, d), jnp.float32),
        compiler_params=_params("parallel"),
        name="ffn",
    )(x, g, wg, wu, wd, final_g)


def _norm_proj_kernel(x_ref, g_ref, w_ref, o_ref):
    n = _rms(x_ref[...], g_ref[...]).astype(jnp.bfloat16)
    o_ref[...] = _dot(n, w_ref[...]).astype(o_ref.dtype)


def _norm_proj(x, g, w, *, tm=512):
    t, d = x.shape
    n_out = w.shape[1]
    assert t % tm == 0
    return pl.pallas_call(
        _norm_proj_kernel,
        grid=(t // tm,),
        in_specs=[pl.BlockSpec((tm, d), lambda i: (i, 0)), _resident((1, d)), _resident((d, n_out))],
        out_specs=pl.BlockSpec((tm, n_out), lambda i: (i, 0)),
        out_shape=jax.ShapeDtypeStruct((t, n_out), jnp.bfloat16),
        compiler_params=_params("parallel"),
        name="norm_proj",
    )(x, g, w)


def _moba_proj_kernel(x_ref, g_ref, wqk_ref, wvt_ref, qk_ref, vt_ref):
    n = _rms(x_ref[...], g_ref[...]).astype(jnp.bfloat16)
    qk_ref[...] = _dot(n, wqk_ref[...]).astype(qk_ref.dtype)
    vt = _dot_nt(wvt_ref[...], n).astype(vt_ref.dtype)
    dh, blk = MOBA_HEAD_DIM, MOBA_BLOCK
    for h in range(MOBA_KV_HEADS):
        for c in range(vt.shape[1] // blk):
            vt_ref[0, h, c] = vt[h * dh:(h + 1) * dh, c * blk:(c + 1) * blk]


def _moba_proj(x, g, w_qk, w_vt, *, seq, tm=512):
    t, d = x.shape
    n_qk = w_qk.shape[1]
    blk, dh = MOBA_BLOCK, MOBA_HEAD_DIM
    assert seq % tm == 0 and tm % blk == 0
    tiles_per_seq = seq // tm
    return pl.pallas_call(
        _moba_proj_kernel,
        grid=(t // tm,),
        in_specs=[pl.BlockSpec((tm, d), lambda i: (i, 0)), _resident((1, d)), _resident(w_qk.shape),
                  _resident(w_vt.shape)],
        out_specs=[
            pl.BlockSpec((tm, n_qk), lambda i: (i, 0)),
            pl.BlockSpec((1, MOBA_KV_HEADS, tm // blk, dh, blk),
                         lambda i: (i // tiles_per_seq, 0, i % tiles_per_seq, 0, 0)),
        ],
        out_shape=[
            jax.ShapeDtypeStruct((t, n_qk), jnp.bfloat16),
            jax.ShapeDtypeStruct((t // seq, MOBA_KV_HEADS, seq // blk, dh, blk), jnp.bfloat16),
        ],
        compiler_params=_params("parallel"),
        name="moba_proj",
    )(x, g, w_qk, w_vt)


def _proj_residual_kernel(x_ref, y_ref, w_ref, o_ref):
    o_ref[...] = x_ref[...] + _dot(y_ref[...], w_ref[...])


def _proj_residual(x, y, w, *, tm=512):
    t, d = x.shape
    k = y.shape[1]
    assert t % tm == 0
    return pl.pallas_call(
        _proj_residual_kernel,
        grid=(t // tm,),
        in_specs=[pl.BlockSpec((tm, d), lambda i: (i, 0)), pl.BlockSpec((tm, k), lambda i: (i, 0)),
                  _resident((k, d))],
        out_specs=pl.BlockSpec((tm, d), lambda i: (i, 0)),
        out_shape=jax.ShapeDtypeStruct((t, d), jnp.float32),
        compiler_params=_params("parallel"),
        name="proj_residual",
    )(x, y, w)


def _rel_bucket(dist):
    n = np.maximum(dist, 0)
    max_exact = REL_BUCKETS // 2
    nf = np.maximum(n, 1).astype(np.float32)
    scaled = (np.log(nf / np.float32(max_exact)) / np.float32(math.log(REL_MAX_DIST / max_exact))
              * np.float32(REL_BUCKETS - max_exact))
    large = np.minimum(max_exact + scaled.astype(np.int32), REL_BUCKETS - 1)
    return np.where(n < max_exact, n, large).astype(np.int32)


def _bias_tile(table, dist, visible):
    bucket = _rel_bucket(dist)
    bias = jnp.full((table.shape[1],) + dist.shape, MASKED, jnp.float32)
    for b in np.unique(bucket[visible]):
        bias = jnp.where((bucket == b) & visible, table[b][:, None, None], bias)
    return bias


def _even_mixer_kernel(sinks_ref, x_ref, z_ref, kvh_ref, ch_ref, uh_ref, convw_ref, bias_ref,
                       wout_ref, o_ref, y_ref, *, halo_rows):
    i = pl.program_id(1)
    ts = z_ref.shape[1]
    c1, c2, c3 = CONV_CH, 2 * CONV_CH, 3 * CONV_CH
    q0 = c3
    k0 = q0 + SWA_HEADS * SWA_HEAD_DIM
    v0 = k0 + SWA_KV_HEADS * SWA_HEAD_DIM
    group = SWA_HEADS // SWA_KV_HEADS
    dh = SWA_HEAD_DIM
    first = i == 0

    f32 = jnp.float32
    v = z_ref[0, :, c1:c2].astype(f32) * z_ref[0, :, c2:c3].astype(f32)
    vh = ch_ref[0].astype(f32) * uh_ref[0].astype(f32)
    vh = jnp.where(first, 0.0, vh)
    row = lax.broadcasted_iota(jnp.int32, v.shape, 0)
    h1 = vh[halo_rows - 1:halo_rows, :]
    h2 = vh[halo_rows - 2:halo_rows - 1, :]
    v1 = jnp.where(row == 0, h1, pltpu.roll(v, 1, 0))
    v2 = jnp.where(row == 0, h2, jnp.where(row == 1, h1, pltpu.roll(v, 2, 0)))
    conv = v2 * convw_ref[0:1, :] + v1 * convw_ref[1:2, :] + v * convw_ref[2:3, :]
    y_ref[:, 0:CONV_CH] = (z_ref[0, :, 0:c1].astype(f32) * conv).astype(y_ref.dtype)

    scale = dh ** -0.5
    col = lax.broadcasted_iota(jnp.int32, (WINDOW, 2 * WINDOW), 1)
    for qb in range(ts // WINDOW):
        rows = slice(qb * WINDOW, (qb + 1) * WINDOW)
        for kh in range(SWA_KV_HEADS):
            kcol = slice(k0 + kh * dh, k0 + (kh + 1) * dh)
            vcol = slice(v0 + kh * dh, v0 + (kh + 1) * dh)
            if qb == 0:
                k_prev = kvh_ref[0, :, kh * dh:(kh + 1) * dh]
                v_prev = kvh_ref[0, :, (SWA_KV_HEADS + kh) * dh:(SWA_KV_HEADS + kh + 1) * dh]
            else:
                prev_rows = slice((qb - 1) * WINDOW, qb * WINDOW)
                k_prev = z_ref[0, prev_rows, kcol]
                v_prev = z_ref[0, prev_rows, vcol]
            kc = jnp.concatenate([k_prev, z_ref[0, rows, kcol]], axis=0)
            vc = jnp.concatenate([v_prev, z_ref[0, rows, vcol]], axis=0)
            qs = jnp.concatenate(
                [z_ref[0, rows, q0 + (kh * group + g) * dh:q0 + (kh * group + g + 1) * dh]
                 for g in range(group)], axis=0)
            logits = _dot_nt(qs, kc) * scale + bias_ref[kh]
            for g in range(group):
                head = kh * group + g
                lg = logits[g * WINDOW:(g + 1) * WINDOW]
                if qb == 0:
                    lg = jnp.where(first & (col < WINDOW), MASKED, lg)
                sink = sinks_ref[head]
                m = jnp.maximum(jnp.max(lg, axis=-1, keepdims=True), sink)
                p = jnp.exp(lg - m)
                denom = jnp.sum(p, axis=-1, keepdims=True) + jnp.exp(sink - m)
                out = _dot(p.astype(jnp.bfloat16), vc) / denom
                y_ref[rows, CONV_CH + head * dh:CONV_CH + (head + 1) * dh] = out.astype(y_ref.dtype)

    o_ref[0] = x_ref[0] + _dot(y_ref[...], wout_ref[...])


def _even_mixer(x, z, conv_w, sinks, w_out, table, *, ts=512, halo_rows=16):
    b, s, d = x.shape
    n_in = z.shape[-1]
    assert s % ts == 0 and ts % WINDOW == 0
    dist = np.arange(WINDOW)[:, None] + WINDOW - np.arange(2 * WINDOW)[None, :]
    bias = _bias_tile(table, dist, (dist >= 0) & (dist < WINDOW))
    bias = bias.reshape(SWA_KV_HEADS, (SWA_HEADS // SWA_KV_HEADS) * WINDOW, 2 * WINDOW)
    kv_w = 2 * SWA_KV_HEADS * SWA_HEAD_DIM
    kv_block = (3 * CONV_CH + SWA_HEADS * SWA_HEAD_DIM) // kv_w
    w_per_tile = ts // WINDOW
    h_per_tile = ts // halo_rows
    return pl.pallas_call(
        functools.partial(_even_mixer_kernel, halo_rows=halo_rows),
        grid=(b, s // ts),
        in_specs=[
            pl.BlockSpec(memory_space=pltpu.SMEM),
            pl.BlockSpec((1, ts, d), lambda bi, i: (bi, i, 0)),
            pl.BlockSpec((1, ts, n_in), lambda bi, i: (bi, i, 0)),
            pl.BlockSpec((1, WINDOW, kv_w), lambda bi, i: (bi, jnp.maximum(i * w_per_tile - 1, 0), kv_block)),
            pl.BlockSpec((1, halo_rows, CONV_CH), lambda bi, i: (bi, jnp.maximum(i * h_per_tile - 1, 0), 1)),
            pl.BlockSpec((1, halo_rows, CONV_CH), lambda bi, i: (bi, jnp.maximum(i * h_per_tile - 1, 0), 2)),
            _resident(conv_w.shape),
            _resident(bias.shape),
            _resident(w_out.shape),
        ],
        out_specs=pl.BlockSpec((1, ts, d), lambda bi, i: (bi, i, 0)),
        out_shape=jax.ShapeDtypeStruct((b, s, d), jnp.float32),
        scratch_shapes=[pltpu.VMEM((ts, w_out.shape[0]), jnp.bfloat16)],
        compiler_params=_params("parallel", "parallel"),
        name="even_mixer",
    )(sinks, x, z, z, z, z, conv_w, bias, w_out)


def _moba_kernel(q_ref, k_ref, vt_ref, bias_ref, o_ref,
                 kaug_ref, kmean_ref, gate_ref, neg_ref, qaug_ref, e_ref, p_ref, acc_ref,
                 *, step_blocks):
    i = pl.program_id(2)
    blk = MOBA_BLOCK
    dh = MOBA_HEAD_DIM
    nblk = k_ref.shape[1] // blk
    group = q_ref.shape[2] // dh
    rows = group * blk
    f32, bf16 = jnp.float32, jnp.bfloat16
    c1 = dh ** -0.5 * LOG2E

    @pl.when(i == 0)
    def _():
        onehot_row = lax.broadcasted_iota(jnp.int32, (blk, dh), 1)
        means = []
        for j in range(nblk):
            kj = k_ref[0, j * blk:(j + 1) * blk, :]
            kaug_ref[j * blk:(j + 1) * blk, 0:dh] = kj
            kaug_ref[j * blk:(j + 1) * blk, dh:2 * dh] = (onehot_row == j).astype(bf16)
            means.append(jnp.mean(kj.astype(f32), axis=0, keepdims=True))
        rest = jnp.concatenate(means, axis=0)
        for part in range(3):
            term = rest.astype(bf16)
            kmean_ref[part * nblk:(part + 1) * nblk, :] = term
            rest = rest - term.astype(f32)
        neg_ref[...] = jnp.zeros(neg_ref.shape, f32)

    q = q_ref[0]
    qs = jnp.concatenate([q[:, g * dh:(g + 1) * dh] for g in range(group)], axis=0)
    qaug_ref[:, 0:dh] = qs

    gate3 = _dot_nt(kmean_ref[...], qs)
    gate = gate3[0:nblk] + gate3[nblk:2 * nblk] + gate3[2 * nblk:3 * nblk]
    gate_ref[...] = gate
    blkid = lax.broadcasted_iota(jnp.int32, gate.shape, 0)
    past = blkid < i
    neg_ref[0:nblk, :] = jnp.zeros((nblk, rows), f32)

    def select(j, carry):
        gj = gate_ref[pl.ds(j, 1), :]
        beats = ((gate > gj) | ((gate == gj) & (blkid < j))) & past
        n_beat = jnp.sum(beats.astype(f32), axis=0, keepdims=True)
        neg_ref[pl.ds(j, 1), :] = jnp.where(n_beat < MOBA_TOPK, 0.0, MASKED)
        return carry

    lax.fori_loop(0, i, select, 0)
    qaug_ref[:, dh:2 * dh] = neg_ref[...].T.astype(bf16)
    qaug = qaug_ref[...]

    def window_start(u):
        return jnp.maximum(i - step_blocks * u - (step_blocks - 1), 0)

    def exponents(u):
        top = i - step_blocks * u
        lo = window_start(u)
        s = _dot_nt(kaug_ref[pl.ds(pl.multiple_of(lo * blk, blk), step_blocks * blk), :], qaug)
        tiles = [bias_ref[0, jnp.where(lo + r > top, 3, jnp.minimum(i - lo - r, 2))]
                 for r in range(step_blocks)]
        e = s * c1 + jnp.concatenate(tiles, axis=0)
        return e, jnp.max(e, axis=0, keepdims=True)

    def weighted_values(lo, p):
        vt = jnp.concatenate([vt_ref[0, 0, lo + r] for r in range(step_blocks)], axis=1)
        return _dot(vt, p)

    def step(u, carry):
        alpha_prev, m, l, e_max, lo_prev = carry
        slot = u % 2
        acc_ref[...] = alpha_prev * acc_ref[...] + weighted_values(lo_prev, p_ref[1 - slot])
        m_new = jnp.maximum(m, e_max)
        alpha = jnp.exp2(m - m_new)
        p = jnp.exp2(e_ref[slot] - m_new)
        l = alpha * l + jnp.sum(p, axis=0, keepdims=True)
        p_ref[slot] = p.astype(bf16)
        e_next, e_max_next = exponents(u + 1)
        e_ref[1 - slot] = e_next
        return alpha, m_new, l, e_max_next, window_start(u)

    n_steps = (i + step_blocks) // step_blocks
    e_first, e_max_first = exponents(0)
    e_ref[0] = e_first
    p_ref[1] = jnp.zeros(p_ref.shape[1:], bf16)
    acc_ref[...] = jnp.zeros((dh, rows), f32)
    init = (jnp.ones((1, rows), f32), jnp.full((1, rows), MASKED, f32), jnp.zeros((1, rows), f32),
            e_max_first, jnp.int32(0))
    alpha_last, _, l, _, lo_last = lax.fori_loop(0, n_steps, step, init)
    acc = alpha_last * acc_ref[...] + weighted_values(lo_last, p_ref[(n_steps - 1) % 2])

    out = (acc / l).T.astype(o_ref.dtype)
    for g in range(group):
        o_ref[0, :, g * dh:(g + 1) * dh] = out[g * blk:(g + 1) * blk]


def _moba(qk, vt, table, *, step_blocks=2):
    b, s, _ = qk.shape
    blk, dh = MOBA_BLOCK, MOBA_HEAD_DIM
    assert s % blk == 0 and dh == V7X_LANES
    nblk = s // blk
    assert step_blocks <= nblk <= V7X_LANES
    keys = step_blocks * blk
    group = MOBA_HEADS // MOBA_KV_HEADS
    rows = group * blk
    qi = np.arange(blk)[None, :]
    kj = np.arange(blk)[:, None]

    def transposed_tile(dist, visible):
        tile = _bias_tile(table, dist, visible) * LOG2E
        tile = tile.reshape(MOBA_KV_HEADS, group, blk, blk)
        return jnp.moveaxis(tile, 1, 2).reshape(MOBA_KV_HEADS, blk, rows)

    everything = np.ones((blk, blk), bool)
    nothing = np.zeros((blk, blk), bool)
    bias = jnp.stack([transposed_tile(qi - kj, qi - kj >= 0),
                      transposed_tile(qi + blk - kj, everything),
                      transposed_tile(qi + 2 * blk - kj, everything),
                      transposed_tile(qi - kj, nothing)], axis=1)
    assert (_rel_bucket(np.arange(blk + 1, 2 * blk * nblk)) == REL_BUCKETS - 1).all()
    k_block0 = MOBA_HEADS
    return pl.pallas_call(
        functools.partial(_moba_kernel, step_blocks=step_blocks),
        grid=(b, MOBA_KV_HEADS, nblk),
        in_specs=[
            pl.BlockSpec((1, blk, group * dh), lambda bi, h, i: (bi, i, h)),
            pl.BlockSpec((1, s, dh), lambda bi, h, i: (bi, 0, k_block0 + h)),
            pl.BlockSpec((1, 1, nblk, dh, blk), lambda bi, h, i: (bi, h, 0, 0, 0)),
            pl.BlockSpec((1, 4, blk, rows), lambda bi, h, i: (h, 0, 0, 0)),
        ],
        out_specs=pl.BlockSpec((1, blk, group * dh), lambda bi, h, i: (bi, i, h)),
        out_shape=jax.ShapeDtypeStruct((b, s, MOBA_HEADS * dh), jnp.bfloat16),
        scratch_shapes=[
            pltpu.VMEM((s, 2 * dh), jnp.bfloat16),
            pltpu.VMEM((3 * nblk, dh), jnp.bfloat16),
            pltpu.VMEM((nblk, rows), jnp.float32),
            pltpu.VMEM((V7X_LANES, rows), jnp.float32),
            pltpu.VMEM((rows, 2 * dh), jnp.bfloat16),
            pltpu.VMEM((2, keys, rows), jnp.float32),
            pltpu.VMEM((2, keys, rows), jnp.bfloat16),
            pltpu.VMEM((dh, rows), jnp.float32),
        ],
        compiler_params=_params("parallel", "parallel", "arbitrary"),
        name="moba",
    )(qk, qk, vt, bias)


def _xattn_kernel(x_ref, g_ref, wq_ref, kv_ref, wo_ref, o_ref, a_ref):
    dh = XA_HEAD_DIM
    x = x_ref[...]
    n = _rms(x, g_ref[...]).astype(jnp.bfloat16)
    q = _dot(n, wq_ref[...]).astype(jnp.bfloat16)
    scale = dh ** -0.5
    v0 = XA_HEADS * dh
    for h in range(XA_HEADS):
        cols = slice(h * dh, (h + 1) * dh)
        s = _dot_nt(q[:, cols], kv_ref[:, cols]) * scale
        m = jnp.max(s, axis=-1, keepdims=True)
        p = jnp.exp(s - m)
        l = jnp.sum(p, axis=-1, keepdims=True)
        out = _dot(p.astype(jnp.bfloat16), kv_ref[:, v0 + h * dh:v0 + (h + 1) * dh]) / l
        a_ref[:, cols] = out.astype(a_ref.dtype)
    o_ref[...] = x + _dot(a_ref[...], wo_ref[...])


def _xattn(x, g, w_q, kv, layer, w_o, *, seq, mem_len, tm=512):
    t, d = x.shape
    xa_w = XA_HEADS * XA_HEAD_DIM
    assert seq % tm == 0
    tiles_per_seq = seq // tm
    return pl.pallas_call(
        _xattn_kernel,
        grid=(t // tm,),
        in_specs=[
            pl.BlockSpec((tm, d), lambda i: (i, 0)),
            _resident((1, d)),
            _resident((d, xa_w)),
            pl.BlockSpec((mem_len, 2 * xa_w), lambda i: (i // tiles_per_seq, layer)),
            _resident((xa_w, d)),
        ],
        out_specs=pl.BlockSpec((tm, d), lambda i: (i, 0)),
        out_shape=jax.ShapeDtypeStruct((t, d), jnp.float32),
        scratch_shapes=[pltpu.VMEM((tm, xa_w), jnp.bfloat16)],
        compiler_params=_params("parallel"),
        name="xattn",
    )(x, g, w_q, kv, w_o)


def kernel(x, mem, ffn1_norm, ffn1_w_gate, ffn1_w_up, ffn1_w_down, mix_norm, ev_w_in, ev_conv_w,
           ev_sinks, ev_w_out, od_w_in, od_w_out, rel_bias, xa_norm, xa_w_q, xa_w_kv, xa_w_o,
           mem_norm, ffn2_norm, ffn2_w_gate, ffn2_w_up, ffn2_w_down, final_norm):
    b, s, d = x.shape
    depth = ffn1_norm.shape[0]
    mem_len = mem.shape[1]
    bf16 = jnp.bfloat16
    row = lambda g: g.reshape(1, d)

    kv_w = jnp.concatenate([xa_w_kv[l] for l in range(depth)], axis=1).astype(bf16)
    kv = _norm_proj(mem.reshape(b * mem_len, d), row(mem_norm), kv_w, tm=min(512, b * mem_len))

    xt = x.reshape(b * s, d)
    for l in range(depth):
        xt = _ffn(xt, row(ffn1_norm[l]), ffn1_w_gate[l].astype(bf16), ffn1_w_up[l].astype(bf16),
                  ffn1_w_down[l].astype(bf16), row(final_norm), final_norm=False)
        i = l // 2
        if l % 2 == 0:
            z = _norm_proj(xt, row(mix_norm[l]), ev_w_in[i].astype(bf16))
            xt = _even_mixer(xt.reshape(b, s, d), z.reshape(b, s, -1), ev_conv_w[i], ev_sinks[i],
                             ev_w_out[i].astype(bf16), rel_bias).reshape(b * s, d)
        else:
            qk_w = MOBA_HEADS * MOBA_HEAD_DIM + MOBA_KV_HEADS * MOBA_HEAD_DIM
            w_in = od_w_in[i].astype(bf16)
            qk, vt = _moba_proj(xt, row(mix_norm[l]), w_in[:, :qk_w], w_in[:, qk_w:].T, seq=s)
            y = _moba(qk.reshape(b, s, -1), vt, rel_bias)
            xt = _proj_residual(xt, y.reshape(b * s, -1), od_w_out[i].astype(bf16))
        xt = _xattn(xt, row(xa_norm[l]), xa_w_q[l].astype(bf16), kv, l, xa_w_o[l].astype(bf16),
                    seq=s, mem_len=mem_len)
        xt = _ffn(xt, row(ffn2_norm[l]), ffn2_w_gate[l].astype(bf16), ffn2_w_up[l].astype(bf16),
                  ffn2_w_down[l].astype(bf16), row(final_norm), final_norm=(l == depth - 1))
    return xt.reshape(b, s, d)
```

```python
import functools
import math

import jax
import jax.numpy as jnp
import numpy as np
from jax import lax
from jax.experimental import pallas as pl
from jax.experimental.pallas import tpu as pltpu

EPS = 1e-6
CONV_WIDTH = 3
CONV_CH = 512
SWA_HEADS = 8
SWA_KV_HEADS = 2
SWA_HEAD_DIM = 64
WINDOW = 128
MOBA_HEADS = 8
MOBA_KV_HEADS = 4
MOBA_HEAD_DIM = 128
MOBA_BLOCK = 256
MOBA_TOPK = 3
REL_BUCKETS = 32
REL_MAX_DIST = 128
XA_HEADS = 4
XA_HEAD_DIM = 128

V7X_VMEM_BYTES = 64 * 1024 * 1024
VMEM_LIMIT_BYTES = V7X_VMEM_BYTES * 3 // 4
V7X_LANES = 128
MASKED = -1e30
LOG2E = math.log2(math.e)

_NT = (((1,), (1,)), ((), ()))


def _params(*semantics):
    return pltpu.CompilerParams(dimension_semantics=semantics, vmem_limit_bytes=VMEM_LIMIT_BYTES)


def _resident(shape):
    zeros = (0,) * len(shape)
    return pl.BlockSpec(shape, lambda *_: zeros, pipeline_mode=pl.Buffered(1))


def _rms(x, g):
    return x * lax.rsqrt(jnp.mean(x * x, axis=-1, keepdims=True) + EPS) * g


def _dot(a, b):
    return jnp.dot(a, b, preferred_element_type=jnp.float32)


def _dot_nt(a, b):
    return lax.dot_general(a, b, _NT, preferred_element_type=jnp.float32)


def _ffn_kernel(x_ref, g_ref, wg_ref, wu_ref, wd_ref, fg_ref, o_ref, *, ff_chunk, final_norm):
    x = x_ref[...]
    n = _rms(x, g_ref[...]).astype(jnp.bfloat16)
    d_ff = wg_ref.shape[1]
    acc = jnp.zeros(x.shape, jnp.float32)
    for c in range(d_ff // ff_chunk):
        cols = slice(c * ff_chunk, (c + 1) * ff_chunk)
        gate = _dot(n, wg_ref[:, cols])
        up = _dot(n, wu_ref[:, cols])
        h = (gate * jax.nn.sigmoid(gate) * up).astype(jnp.bfloat16)
        acc = acc + _dot(h, wd_ref[cols, :])
    y = x + 0.5 * acc
    if final_norm:
        y = _rms(y, fg_ref[...])
    o_ref[...] = y


def _ffn(x, g, wg, wu, wd, final_g, *, final_norm, tm=512, ff_chunk=256):
    t, d = x.shape
    d_ff = wg.shape[1]
    assert t % tm == 0 and d_ff % ff_chunk == 0
    row = pl.BlockSpec((tm, d), lambda i: (i, 0))
    return pl.pallas_call(
        functools.partial(_ffn_kernel, ff_chunk=ff_chunk, final_norm=final_norm),
        grid=(t // tm,),
        in_specs=[row, _resident((1, d)), _resident((d, d_ff)), _resident((d, d_ff)),
                  _resident((d_ff, d)), _resident((1, d))],
        out_specs=row,
        out_shape=jax.ShapeDtypeStruct((t, d), jnp.float32),
        compiler_params=_params("parallel"),
        name="ffn",
    )(x, g, wg, wu, wd, final_g)


def _norm_proj_kernel(x_ref, g_ref, w_ref, o_ref):
    n = _rms(x_ref[...], g_ref[...]).astype(jnp.bfloat16)
    o_ref[...] = _dot(n, w_ref[...]).astype(o_ref.dtype)


def _norm_proj(x, g, w, *, tm=512):
    t, d = x.shape
    n_out = w.shape[1]
    assert t % tm == 0
    return pl.pallas_call(
        _norm_proj_kernel,
        grid=(t // tm,),
        in_specs=[pl.BlockSpec((tm, d), lambda i: (i, 0)), _resident((1, d)), _resident((d, n_out))],
        out_specs=pl.BlockSpec((tm, n_out), lambda i: (i, 0)),
        out_shape=jax.ShapeDtypeStruct((t, n_out), jnp.bfloat16),
        compiler_params=_params("parallel"),
        name="norm_proj",
    )(x, g, w)


def _moba_proj_kernel(x_ref, g_ref, wqk_ref, wvt_ref, qk_ref, vt_ref):
    n = _rms(x_ref[...], g_ref[...]).astype(jnp.bfloat16)
    qk_ref[...] = _dot(n, wqk_ref[...]).astype(qk_ref.dtype)
    vt = _dot_nt(wvt_ref[...], n).astype(vt_ref.dtype)
    dh = MOBA_HEAD_DIM
    for h in range(MOBA_KV_HEADS):
        vt_ref[0, h] = vt[h * dh:(h + 1) * dh, :]


def _moba_proj(x, g, w_qk, w_vt, *, seq, tm=512):
    t, d = x.shape
    n_qk = w_qk.shape[1]
    dh = MOBA_HEAD_DIM
    assert seq % tm == 0
    tiles_per_seq = seq // tm
    return pl.pallas_call(
        _moba_proj_kernel,
        grid=(t // tm,),
        in_specs=[pl.BlockSpec((tm, d), lambda i: (i, 0)), _resident((1, d)), _resident(w_qk.shape),
                  _resident(w_vt.shape)],
        out_specs=[
            pl.BlockSpec((tm, n_qk), lambda i: (i, 0)),
            pl.BlockSpec((1, MOBA_KV_HEADS, dh, tm),
                         lambda i: (i // tiles_per_seq, 0, 0, i % tiles_per_seq)),
        ],
        out_shape=[
            jax.ShapeDtypeStruct((t, n_qk), jnp.bfloat16),
            jax.ShapeDtypeStruct((t // seq, MOBA_KV_HEADS, dh, seq), jnp.bfloat16),
        ],
        compiler_params=_params("parallel"),
        name="moba_proj",
    )(x, g, w_qk, w_vt)


def _proj_residual_kernel(x_ref, y_ref, w_ref, o_ref):
    o_ref[...] = x_ref[...] + _dot(y_ref[...], w_ref[...])


def _proj_residual(x, y, w, *, tm=512):
    t, d = x.shape
    k = y.shape[1]
    assert t % tm == 0
    return pl.pallas_call(
        _proj_residual_kernel,
        grid=(t // tm,),
        in_specs=[pl.BlockSpec((tm, d), lambda i: (i, 0)), pl.BlockSpec((tm, k), lambda i: (i, 0)),
                  _resident((k, d))],
        out_specs=pl.BlockSpec((tm, d), lambda i: (i, 0)),
        out_shape=jax.ShapeDtypeStruct((t, d), jnp.float32),
        compiler_params=_params("parallel"),
        name="proj_residual",
    )(x, y, w)


def _rel_bucket(dist):
    n = np.maximum(dist, 0)
    max_exact = REL_BUCKETS // 2
    nf = np.maximum(n, 1).astype(np.float32)
    scaled = (np.log(nf / np.float32(max_exact)) / np.float32(math.log(REL_MAX_DIST / max_exact))
              * np.float32(REL_BUCKETS - max_exact))
    large = np.minimum(max_exact + scaled.astype(np.int32), REL_BUCKETS - 1)
    return np.where(n < max_exact, n, large).astype(np.int32)


def _bias_tile(table, dist, visible):
    bucket = _rel_bucket(dist)
    bias = jnp.full((table.shape[1],) + dist.shape, MASKED, jnp.float32)
    for b in np.unique(bucket[visible]):
        bias = jnp.where((bucket == b) & visible, table[b][:, None, None], bias)
    return bias


def _even_mixer_kernel(sinks_ref, x_ref, z_ref, kvh_ref, ch_ref, uh_ref, convw_ref, bias_ref,
                       wout_ref, o_ref, y_ref, *, halo_rows):
    i = pl.program_id(1)
    ts = z_ref.shape[1]
    c1, c2, c3 = CONV_CH, 2 * CONV_CH, 3 * CONV_CH
    q0 = c3
    k0 = q0 + SWA_HEADS * SWA_HEAD_DIM
    v0 = k0 + SWA_KV_HEADS * SWA_HEAD_DIM
    group = SWA_HEADS // SWA_KV_HEADS
    dh = SWA_HEAD_DIM
    first = i == 0

    f32 = jnp.float32
    v = z_ref[0, :, c1:c2].astype(f32) * z_ref[0, :, c2:c3].astype(f32)
    vh = ch_ref[0].astype(f32) * uh_ref[0].astype(f32)
    vh = jnp.where(first, 0.0, vh)
    row = lax.broadcasted_iota(jnp.int32, v.shape, 0)
    h1 = vh[halo_rows - 1:halo_rows, :]
    h2 = vh[halo_rows - 2:halo_rows - 1, :]
    v1 = jnp.where(row == 0, h1, pltpu.roll(v, 1, 0))
    v2 = jnp.where(row == 0, h2, jnp.where(row == 1, h1, pltpu.roll(v, 2, 0)))
    conv = v2 * convw_ref[0:1, :] + v1 * convw_ref[1:2, :] + v * convw_ref[2:3, :]
    y_ref[:, 0:CONV_CH] = (z_ref[0, :, 0:c1].astype(f32) * conv).astype(y_ref.dtype)

    scale = dh ** -0.5
    col = lax.broadcasted_iota(jnp.int32, (WINDOW, 2 * WINDOW), 1)
    for qb in range(ts // WINDOW):
        rows = slice(qb * WINDOW, (qb + 1) * WINDOW)
        for kh in range(SWA_KV_HEADS):
            kcol = slice(k0 + kh * dh, k0 + (kh + 1) * dh)
            vcol = slice(v0 + kh * dh, v0 + (kh + 1) * dh)
            if qb == 0:
                k_prev = kvh_ref[0, :, kh * dh:(kh + 1) * dh]
                v_prev = kvh_ref[0, :, (SWA_KV_HEADS + kh) * dh:(SWA_KV_HEADS + kh + 1) * dh]
            else:
                prev_rows = slice((qb - 1) * WINDOW, qb * WINDOW)
                k_prev = z_ref[0, prev_rows, kcol]
                v_prev = z_ref[0, prev_rows, vcol]
            kc = jnp.concatenate([k_prev, z_ref[0, rows, kcol]], axis=0)
            vc = jnp.concatenate([v_prev, z_ref[0, rows, vcol]], axis=0)
            qs = jnp.concatenate(
                [z_ref[0, rows, q0 + (kh * group + g) * dh:q0 + (kh * group + g + 1) * dh]
                 for g in range(group)], axis=0)
            logits = _dot_nt(qs, kc) * scale + bias_ref[kh]
            for g in range(group):
                head = kh * group + g
                lg = logits[g * WINDOW:(g + 1) * WINDOW]
                if qb == 0:
                    lg = jnp.where(first & (col < WINDOW), MASKED, lg)
                sink = sinks_ref[head]
                m = jnp.maximum(jnp.max(lg, axis=-1, keepdims=True), sink)
                p = jnp.exp(lg - m)
                denom = jnp.sum(p, axis=-1, keepdims=True) + jnp.exp(sink - m)
                out = _dot(p.astype(jnp.bfloat16), vc) / denom
                y_ref[rows, CONV_CH + head * dh:CONV_CH + (head + 1) * dh] = out.astype(y_ref.dtype)

    o_ref[0] = x_ref[0] + _dot(y_ref[...], wout_ref[...])


def _even_mixer(x, z, conv_w, sinks, w_out, table, *, ts=512, halo_rows=16):
    b, s, d = x.shape
    n_in = z.shape[-1]
    assert s % ts == 0 and ts % WINDOW == 0
    dist = np.arange(WINDOW)[:, None] + WINDOW - np.arange(2 * WINDOW)[None, :]
    bias = _bias_tile(table, dist, (dist >= 0) & (dist < WINDOW))
    bias = bias.reshape(SWA_KV_HEADS, (SWA_HEADS // SWA_KV_HEADS) * WINDOW, 2 * WINDOW)
    kv_w = 2 * SWA_KV_HEADS * SWA_HEAD_DIM
    kv_block = (3 * CONV_CH + SWA_HEADS * SWA_HEAD_DIM) // kv_w
    w_per_tile = ts // WINDOW
    h_per_tile = ts // halo_rows
    return pl.pallas_call(
        functools.partial(_even_mixer_kernel, halo_rows=halo_rows),
        grid=(b, s // ts),
        in_specs=[
            pl.BlockSpec(memory_space=pltpu.SMEM),
            pl.BlockSpec((1, ts, d), lambda bi, i: (bi, i, 0)),
            pl.BlockSpec((1, ts, n_in), lambda bi, i: (bi, i, 0)),
            pl.BlockSpec((1, WINDOW, kv_w), lambda bi, i: (bi, jnp.maximum(i * w_per_tile - 1, 0), kv_block)),
            pl.BlockSpec((1, halo_rows, CONV_CH), lambda bi, i: (bi, jnp.maximum(i * h_per_tile - 1, 0), 1)),
            pl.BlockSpec((1, halo_rows, CONV_CH), lambda bi, i: (bi, jnp.maximum(i * h_per_tile - 1, 0), 2)),
            _resident(conv_w.shape),
            _resident(bias.shape),
            _resident(w_out.shape),
        ],
        out_specs=pl.BlockSpec((1, ts, d), lambda bi, i: (bi, i, 0)),
        out_shape=jax.ShapeDtypeStruct((b, s, d), jnp.float32),
        scratch_shapes=[pltpu.VMEM((ts, w_out.shape[0]), jnp.bfloat16)],
        compiler_params=_params("parallel", "parallel"),
        name="even_mixer",
    )(sinks, x, z, z, z, z, conv_w, bias, w_out)


def _moba_kernel(q_ref, k_ref, vt_ref, bias_ref, o_ref,
                 kaug_ref, kmean_ref, gate_ref, neg_ref, qaug_ref, e_ref,
                 *, case_blocks, chunk_blocks):
    i = pl.program_id(2)
    blk = MOBA_BLOCK
    dh = MOBA_HEAD_DIM
    nblk = k_ref.shape[1] // blk
    group = q_ref.shape[2] // dh
    rows = group * blk
    f32, bf16 = jnp.float32, jnp.bfloat16
    c1 = dh ** -0.5 * LOG2E

    @pl.when(i == 0)
    def _():
        onehot_row = lax.broadcasted_iota(jnp.int32, (blk, dh), 1)
        means = []
        for j in range(nblk):
            kj = k_ref[0, j * blk:(j + 1) * blk, :]
            kaug_ref[j * blk:(j + 1) * blk, 0:dh] = kj
            kaug_ref[j * blk:(j + 1) * blk, dh:2 * dh] = (onehot_row == j).astype(bf16)
            means.append(jnp.mean(kj.astype(f32), axis=0, keepdims=True))
        rest = jnp.concatenate(means, axis=0)
        for part in range(3):
            term = rest.astype(bf16)
            kmean_ref[part * nblk:(part + 1) * nblk, :] = term
            rest = rest - term.astype(f32)
        neg_ref[...] = jnp.zeros(neg_ref.shape, f32)

    q = q_ref[0]
    qs = jnp.concatenate([q[:, g * dh:(g + 1) * dh] for g in range(group)], axis=0)
    qaug_ref[:, 0:dh] = qs

    gate3 = _dot_nt(kmean_ref[...], qs)
    gate = gate3[0:nblk] + gate3[nblk:2 * nblk] + gate3[2 * nblk:3 * nblk]
    gate_ref[...] = gate
    blkid = lax.broadcasted_iota(jnp.int32, gate.shape, 0)
    past = blkid < i
    neg_ref[0:nblk, :] = jnp.zeros((nblk, rows), f32)

    def select(j, carry):
        gj = gate_ref[pl.ds(j, 1), :]
        beats = ((gate > gj) | ((gate == gj) & (blkid < j))) & past
        n_beat = jnp.sum(beats.astype(f32), axis=0, keepdims=True)
        neg_ref[pl.ds(j, 1), :] = jnp.where(n_beat < MOBA_TOPK, 0.0, MASKED)
        return carry

    lax.fori_loop(0, i, select, 0)
    qaug_ref[:, dh:2 * dh] = neg_ref[...].T.astype(bf16)
    qaug = qaug_ref[...]

    def attend(n_keys_blocks):
        m = None
        for c0 in range(0, n_keys_blocks, chunk_blocks):
            cb = min(chunk_blocks, n_keys_blocks - c0)
            s = _dot_nt(kaug_ref[c0 * blk:(c0 + cb) * blk, :], qaug)
            for r in range(cb):
                j = c0 + r
                tile = jnp.where(j > i, 3, jnp.minimum(i - j, 2))
                e = s[r * blk:(r + 1) * blk] * c1 + bias_ref[0, tile]
                e_ref[j * blk:(j + 1) * blk, :] = e
                e_max = jnp.max(e, axis=0, keepdims=True)
                m = e_max if m is None else jnp.maximum(m, e_max)
        l = jnp.zeros((1, rows), f32)
        acc = jnp.zeros((dh, rows), f32)
        for j in range(n_keys_blocks):
            p = jnp.exp2(e_ref[j * blk:(j + 1) * blk, :] - m)
            l = l + jnp.sum(p, axis=0, keepdims=True)
            acc = acc + _dot(vt_ref[0, 0, :, j * blk:(j + 1) * blk], p.astype(bf16))
        out = (acc / l).T.astype(o_ref.dtype)
        for g in range(group):
            o_ref[0, :, g * dh:(g + 1) * dh] = out[g * blk:(g + 1) * blk]

    for case in range(nblk // case_blocks):
        @pl.when(i // case_blocks == case)
        def _(case=case):
            attend(case_blocks * (case + 1))


def _moba(qk, vt, table, *, case_blocks=2, chunk_blocks=4):
    b, s, _ = qk.shape
    blk, dh = MOBA_BLOCK, MOBA_HEAD_DIM
    assert s % blk == 0 and dh == V7X_LANES
    nblk = s // blk
    assert nblk % case_blocks == 0 and nblk <= V7X_LANES
    group = MOBA_HEADS // MOBA_KV_HEADS
    rows = group * blk
    qi = np.arange(blk)[None, :]
    kj = np.arange(blk)[:, None]

    def transposed_tile(dist, visible):
        tile = _bias_tile(table, dist, visible) * LOG2E
        tile = tile.reshape(MOBA_KV_HEADS, group, blk, blk)
        return jnp.moveaxis(tile, 1, 2).reshape(MOBA_KV_HEADS, blk, rows)

    everything = np.ones((blk, blk), bool)
    nothing = np.zeros((blk, blk), bool)
    bias = jnp.stack([transposed_tile(qi - kj, qi - kj >= 0),
                      transposed_tile(qi + blk - kj, everything),
                      transposed_tile(qi + 2 * blk - kj, everything),
                      transposed_tile(qi - kj, nothing)], axis=1)
    assert (_rel_bucket(np.arange(blk + 1, 2 * blk * nblk)) == REL_BUCKETS - 1).all()
    k_block0 = MOBA_HEADS
    return pl.pallas_call(
        functools.partial(_moba_kernel, case_blocks=case_blocks, chunk_blocks=chunk_blocks),
        grid=(b, MOBA_KV_HEADS, nblk),
        in_specs=[
            pl.BlockSpec((1, blk, group * dh), lambda bi, h, i: (bi, i, h)),
            pl.BlockSpec((1, s, dh), lambda bi, h, i: (bi, 0, k_block0 + h)),
            pl.BlockSpec((1, 1, dh, s), lambda bi, h, i: (bi, h, 0, 0)),
            pl.BlockSpec((1, 4, blk, rows), lambda bi, h, i: (h, 0, 0, 0)),
        ],
        out_specs=pl.BlockSpec((1, blk, group * dh), lambda bi, h, i: (bi, i, h)),
        out_shape=jax.ShapeDtypeStruct((b, s, MOBA_HEADS * dh), jnp.bfloat16),
        scratch_shapes=[
            pltpu.VMEM((s, 2 * dh), jnp.bfloat16),
            pltpu.VMEM((3 * nblk, dh), jnp.bfloat16),
            pltpu.VMEM((nblk, rows), jnp.float32),
            pltpu.VMEM((V7X_LANES, rows), jnp.float32),
            pltpu.VMEM((rows, 2 * dh), jnp.bfloat16),
            pltpu.VMEM((s, rows), jnp.float32),
        ],
        compiler_params=_params("parallel", "parallel", "arbitrary"),
        name="moba",
    )(qk, qk, vt, bias)


def _xattn_kernel(x_ref, g_ref, wq_ref, kv_ref, wo_ref, o_ref, a_ref):
    dh = XA_HEAD_DIM
    x = x_ref[...]
    n = _rms(x, g_ref[...]).astype(jnp.bfloat16)
    q = _dot(n, wq_ref[...]).astype(jnp.bfloat16)
    scale = dh ** -0.5
    v0 = XA_HEADS * dh
    for h in range(XA_HEADS):
        cols = slice(h * dh, (h + 1) * dh)
        s = _dot_nt(q[:, cols], kv_ref[:, cols]) * scale
        m = jnp.max(s, axis=-1, keepdims=True)
        p = jnp.exp(s - m)
        l = jnp.sum(p, axis=-1, keepdims=True)
        out = _dot(p.astype(jnp.bfloat16), kv_ref[:, v0 + h * dh:v0 + (h + 1) * dh]) / l
        a_ref[:, cols] = out.astype(a_ref.dtype)
    o_ref[...] = x + _dot(a_ref[...], wo_ref[...])


def _xattn(x, g, w_q, kv, layer, w_o, *, seq, mem_len, tm=512):
    t, d = x.shape
    xa_w = XA_HEADS * XA_HEAD_DIM
    assert seq % tm == 0
    tiles_per_seq = seq // tm
    return pl.pallas_call(
        _xattn_kernel,
        grid=(t // tm,),
        in_specs=[
            pl.BlockSpec((tm, d), lambda i: (i, 0)),
            _resident((1, d)),
            _resident((d, xa_w)),
            pl.BlockSpec((mem_len, 2 * xa_w), lambda i: (i // tiles_per_seq, layer)),
            _resident((xa_w, d)),
        ],
        out_specs=pl.BlockSpec((tm, d), lambda i: (i, 0)),
        out_shape=jax.ShapeDtypeStruct((t, d), jnp.float32),
        scratch_shapes=[pltpu.VMEM((tm, xa_w), jnp.bfloat16)],
        compiler_params=_params("parallel"),
        name="xattn",
    )(x, g, w_q, kv, w_o)


def kernel(x, mem, ffn1_norm, ffn1_w_gate, ffn1_w_up, ffn1_w_down, mix_norm, ev_w_in, ev_conv_w,
           ev_sinks, ev_w_out, od_w_in, od_w_out, rel_bias, xa_norm, xa_w_q, xa_w_kv, xa_w_o,
           mem_norm, ffn2_norm, ffn2_w_gate, ffn2_w_up, ffn2_w_down, final_norm):
    b, s, d = x.shape
    depth = ffn1_norm.shape[0]
    mem_len = mem.shape[1]
    bf16 = jnp.bfloat16
    row = lambda g: g.reshape(1, d)

    kv_w = jnp.concatenate([xa_w_kv[l] for l in range(depth)], axis=1).astype(bf16)
    kv = _norm_proj(mem.reshape(b * mem_len, d), row(mem_norm), kv_w, tm=min(512, b * mem_len))

    xt = x.reshape(b * s, d)
    for l in range(depth):
        xt = _ffn(xt, row(ffn1_norm[l]), ffn1_w_gate[l].astype(bf16), ffn1_w_up[l].astype(bf16),
                  ffn1_w_down[l].astype(bf16), row(final_norm), final_norm=False)
        i = l // 2
        if l % 2 == 0:
            z = _norm_proj(xt, row(mix_norm[l]), ev_w_in[i].astype(bf16))
            xt = _even_mixer(xt.reshape(b, s, d), z.reshape(b, s, -1), ev_conv_w[i], ev_sinks[i],
                             ev_w_out[i].astype(bf16), rel_bias).reshape(b * s, d)
        else:
            qk_w = MOBA_HEADS * MOBA_HEAD_DIM + MOBA_KV_HEADS * MOBA_HEAD_DIM
            w_in = od_w_in[i].astype(bf16)
            qk, vt = _moba_proj(xt, row(mix_norm[l]), w_in[:, :qk_w], w_in[:, qk_w:].T, seq=s)
            y = _moba(qk.reshape(b, s, -1), vt, rel_bias)
            xt = _proj_residual(xt, y.reshape(b * s, -1), od_w_out[i].astype(bf16))
        xt = _xattn(xt, row(xa_norm[l]), xa_w_q[l].astype(bf16), kv, l, xa_w_o[l].astype(bf16),
                    seq=s, mem_len=mem_len)
        xt = _ffn(xt, row(ffn2_norm[l]), ffn2_w_gate[l].astype(bf16), ffn2_w_up[l].astype(bf16),
                  ffn2_w_down[l].astype(bf16), row(final_norm), final_norm=(l == depth - 1))
    return xt.reshape(b, s, d)
```

```python
import functools
import math

import jax
import jax.numpy as jnp
import numpy as np
from jax import lax
from jax.experimental import pallas as pl
from jax.experimental.pallas import tpu as pltpu

EPS = 1e-6
CONV_WIDTH = 3
CONV_CH = 512
SWA_HEADS = 8
SWA_KV_HEADS = 2
SWA_HEAD_DIM = 64
WINDOW = 128
MOBA_HEADS = 8
MOBA_KV_HEADS = 4
MOBA_HEAD_DIM = 128
MOBA_BLOCK = 256
MOBA_TOPK = 3
MOBA_SUM_ROWS = 16
REL_BUCKETS = 32
REL_MAX_DIST = 128
XA_HEADS = 4
XA_HEAD_DIM = 128

V7X_VMEM_BYTES = 64 * 1024 * 1024
VMEM_LIMIT_BYTES = V7X_VMEM_BYTES * 3 // 4
V7X_LANES = 128
MASKED = -1e30
LOG2E = math.log2(math.e)

_NT = (((1,), (1,)), ((), ()))


def _params(*semantics):
    return pltpu.CompilerParams(dimension_semantics=semantics, vmem_limit_bytes=VMEM_LIMIT_BYTES)


def _resident(shape):
    zeros = (0,) * len(shape)
    return pl.BlockSpec(shape, lambda *_: zeros, pipeline_mode=pl.Buffered(1))


def _rms(x, g):
    return x * lax.rsqrt(jnp.mean(x * x, axis=-1, keepdims=True) + EPS) * g


def _dot(a, b):
    return jnp.dot(a, b, preferred_element_type=jnp.float32)


def _dot_nt(a, b):
    return lax.dot_general(a, b, _NT, preferred_element_type=jnp.float32)


def _half_swiglu(x, g_ref, wg_ref, wu_ref, wd_ref, ff_chunk):
    n = _rms(x, g_ref[...]).astype(jnp.bfloat16)
    d_ff = wg_ref.shape[1]
    acc = jnp.zeros(x.shape, jnp.float32)
    for c in range(d_ff // ff_chunk):
        cols = slice(c * ff_chunk, (c + 1) * ff_chunk)
        gate = _dot(n, wg_ref[:, cols])
        up = _dot(n, wu_ref[:, cols])
        h = (gate * jax.nn.sigmoid(gate) * up).astype(jnp.bfloat16)
        acc = acc + _dot(h, wd_ref[cols, :])
    return x + 0.5 * acc


def _ffn_kernel(x_ref, g_ref, wg_ref, wu_ref, wd_ref, o_ref, *, ff_chunk):
    o_ref[...] = _half_swiglu(x_ref[...], g_ref, wg_ref, wu_ref, wd_ref, ff_chunk)


def _ffn(x, g, wg, wu, wd, *, tm=512, ff_chunk=256):
    t, d = x.shape
    d_ff = wg.shape[1]
    assert t % tm == 0 and d_ff % ff_chunk == 0
    row = pl.BlockSpec((tm, d), lambda i: (i, 0))
    return pl.pallas_call(
        functools.partial(_ffn_kernel, ff_chunk=ff_chunk),
        grid=(t // tm,),
        in_specs=[row, _resident((1, d)), _resident((d, d_ff)), _resident((d, d_ff)),
                  _resident((d_ff, d))],
        out_specs=row,
        out_shape=jax.ShapeDtypeStruct((t, d), jnp.float32),
        compiler_params=_params("parallel"),
        name="ffn",
    )(x, g, wg, wu, wd)


def _norm_proj_kernel(x_ref, g_ref, w_ref, o_ref):
    n = _rms(x_ref[...], g_ref[...]).astype(jnp.bfloat16)
    o_ref[...] = _dot(n, w_ref[...]).astype(o_ref.dtype)


def _norm_proj(x, g, w, *, tm=512):
    t, d = x.shape
    n_out = w.shape[1]
    assert t % tm == 0
    return pl.pallas_call(
        _norm_proj_kernel,
        grid=(t // tm,),
        in_specs=[pl.BlockSpec((tm, d), lambda i: (i, 0)), _resident((1, d)), _resident((d, n_out))],
        out_specs=pl.BlockSpec((tm, n_out), lambda i: (i, 0)),
        out_shape=jax.ShapeDtypeStruct((t, n_out), jnp.bfloat16),
        compiler_params=_params("parallel"),
        name="norm_proj",
    )(x, g, w)


def _moba_proj_kernel(x_ref, g_ref, wqk_ref, wvt_ref, qk_ref, vt_ref):
    n = _rms(x_ref[...], g_ref[...]).astype(jnp.bfloat16)
    qk_ref[...] = _dot(n, wqk_ref[...]).astype(qk_ref.dtype)
    vt = _dot_nt(wvt_ref[...], n).astype(vt_ref.dtype)
    dh = MOBA_HEAD_DIM
    for h in range(MOBA_KV_HEADS):
        vt_ref[0, h, 0:dh] = vt[h * dh:(h + 1) * dh, :]
        vt_ref[0, h, dh:dh + MOBA_SUM_ROWS] = jnp.ones((MOBA_SUM_ROWS, vt.shape[1]), vt_ref.dtype)


def _moba_proj(x, g, w_qk, w_vt, *, seq, tm=512):
    t, d = x.shape
    n_qk = w_qk.shape[1]
    dh = MOBA_HEAD_DIM + MOBA_SUM_ROWS
    assert seq % tm == 0
    tiles_per_seq = seq // tm
    return pl.pallas_call(
        _moba_proj_kernel,
        grid=(t // tm,),
        in_specs=[pl.BlockSpec((tm, d), lambda i: (i, 0)), _resident((1, d)), _resident(w_qk.shape),
                  _resident(w_vt.shape)],
        out_specs=[
            pl.BlockSpec((tm, n_qk), lambda i: (i, 0)),
            pl.BlockSpec((1, MOBA_KV_HEADS, dh, tm),
                         lambda i: (i // tiles_per_seq, 0, 0, i % tiles_per_seq)),
        ],
        out_shape=[
            jax.ShapeDtypeStruct((t, n_qk), jnp.bfloat16),
            jax.ShapeDtypeStruct((t // seq, MOBA_KV_HEADS, dh, seq), jnp.bfloat16),
        ],
        compiler_params=_params("parallel"),
        name="moba_proj",
    )(x, g, w_qk, w_vt)


def _rel_bucket(dist):
    n = np.maximum(dist, 0)
    max_exact = REL_BUCKETS // 2
    nf = np.maximum(n, 1).astype(np.float32)
    scaled = (np.log(nf / np.float32(max_exact)) / np.float32(math.log(REL_MAX_DIST / max_exact))
              * np.float32(REL_BUCKETS - max_exact))
    large = np.minimum(max_exact + scaled.astype(np.int32), REL_BUCKETS - 1)
    return np.where(n < max_exact, n, large).astype(np.int32)


def _bias_tile(table, dist, visible):
    bucket = _rel_bucket(dist)
    bias = jnp.full((table.shape[1],) + dist.shape, MASKED, jnp.float32)
    for b in np.unique(bucket[visible]):
        bias = jnp.where((bucket == b) & visible, table[b][:, None, None], bias)
    return bias


def _even_mixer_kernel(sinks_ref, z_ref, kvh_ref, ch_ref, uh_ref, convw_ref, bias_ref, y_ref,
                       *, halo_rows):
    i = pl.program_id(1)
    ts = z_ref.shape[1]
    c1, c2, c3 = CONV_CH, 2 * CONV_CH, 3 * CONV_CH
    q0 = c3
    k0 = q0 + SWA_HEADS * SWA_HEAD_DIM
    v0 = k0 + SWA_KV_HEADS * SWA_HEAD_DIM
    group = SWA_HEADS // SWA_KV_HEADS
    dh = SWA_HEAD_DIM
    first = i == 0

    f32 = jnp.float32
    v = z_ref[0, :, c1:c2].astype(f32) * z_ref[0, :, c2:c3].astype(f32)
    vh = ch_ref[0].astype(f32) * uh_ref[0].astype(f32)
    vh = jnp.where(first, 0.0, vh)
    row = lax.broadcasted_iota(jnp.int32, v.shape, 0)
    h1 = vh[halo_rows - 1:halo_rows, :]
    h2 = vh[halo_rows - 2:halo_rows - 1, :]
    v1 = jnp.where(row == 0, h1, pltpu.roll(v, 1, 0))
    v2 = jnp.where(row == 0, h2, jnp.where(row == 1, h1, pltpu.roll(v, 2, 0)))
    conv = v2 * convw_ref[0:1, :] + v1 * convw_ref[1:2, :] + v * convw_ref[2:3, :]
    y_ref[0, :, 0:CONV_CH] = (z_ref[0, :, 0:c1].astype(f32) * conv).astype(y_ref.dtype)

    scale = dh ** -0.5
    col = lax.broadcasted_iota(jnp.int32, (WINDOW, 2 * WINDOW), 1)
    for qb in range(ts // WINDOW):
        rows = slice(qb * WINDOW, (qb + 1) * WINDOW)
        for kh in range(SWA_KV_HEADS):
            kcol = slice(k0 + kh * dh, k0 + (kh + 1) * dh)
            vcol = slice(v0 + kh * dh, v0 + (kh + 1) * dh)
            if qb == 0:
                k_prev = kvh_ref[0, :, kh * dh:(kh + 1) * dh]
                v_prev = kvh_ref[0, :, (SWA_KV_HEADS + kh) * dh:(SWA_KV_HEADS + kh + 1) * dh]
            else:
                prev_rows = slice((qb - 1) * WINDOW, qb * WINDOW)
                k_prev = z_ref[0, prev_rows, kcol]
                v_prev = z_ref[0, prev_rows, vcol]
            kc = jnp.concatenate([k_prev, z_ref[0, rows, kcol]], axis=0)
            vc = jnp.concatenate([v_prev, z_ref[0, rows, vcol]], axis=0)
            qs = jnp.concatenate(
                [z_ref[0, rows, q0 + (kh * group + g) * dh:q0 + (kh * group + g + 1) * dh]
                 for g in range(group)], axis=0)
            logits = _dot_nt(qs, kc) * scale + bias_ref[kh]
            for g in range(group):
                head = kh * group + g
                lg = logits[g * WINDOW:(g + 1) * WINDOW]
                if qb == 0:
                    lg = jnp.where(first & (col < WINDOW), MASKED, lg)
                sink = sinks_ref[head]
                m = jnp.maximum(jnp.max(lg, axis=-1, keepdims=True), sink)
                p = jnp.exp(lg - m)
                denom = jnp.sum(p, axis=-1, keepdims=True) + jnp.exp(sink - m)
                out = _dot(p.astype(jnp.bfloat16), vc) / denom
                y_ref[0, rows, CONV_CH + head * dh:CONV_CH + (head + 1) * dh] = out.astype(y_ref.dtype)


def _even_mixer(z, conv_w, sinks, table, *, ts=512, halo_rows=16):
    b, s, n_in = z.shape
    n_out = CONV_CH + SWA_HEADS * SWA_HEAD_DIM
    assert s % ts == 0 and ts % WINDOW == 0
    dist = np.arange(WINDOW)[:, None] + WINDOW - np.arange(2 * WINDOW)[None, :]
    bias = _bias_tile(table, dist, (dist >= 0) & (dist < WINDOW))
    bias = bias.reshape(SWA_KV_HEADS, (SWA_HEADS // SWA_KV_HEADS) * WINDOW, 2 * WINDOW)
    kv_w = 2 * SWA_KV_HEADS * SWA_HEAD_DIM
    kv_block = (3 * CONV_CH + SWA_HEADS * SWA_HEAD_DIM) // kv_w
    w_per_tile = ts // WINDOW
    h_per_tile = ts // halo_rows
    return pl.pallas_call(
        functools.partial(_even_mixer_kernel, halo_rows=halo_rows),
        grid=(b, s // ts),
        in_specs=[
            pl.BlockSpec(memory_space=pltpu.SMEM),
            pl.BlockSpec((1, ts, n_in), lambda bi, i: (bi, i, 0)),
            pl.BlockSpec((1, WINDOW, kv_w), lambda bi, i: (bi, jnp.maximum(i * w_per_tile - 1, 0), kv_block)),
            pl.BlockSpec((1, halo_rows, CONV_CH), lambda bi, i: (bi, jnp.maximum(i * h_per_tile - 1, 0), 1)),
            pl.BlockSpec((1, halo_rows, CONV_CH), lambda bi, i: (bi, jnp.maximum(i * h_per_tile - 1, 0), 2)),
            _resident(conv_w.shape),
            _resident(bias.shape),
        ],
        out_specs=pl.BlockSpec((1, ts, n_out), lambda bi, i: (bi, i, 0)),
        out_shape=jax.ShapeDtypeStruct((b, s, n_out), jnp.bfloat16),
        compiler_params=_params("parallel", "parallel"),
        name="even_mixer",
    )(sinks, z, z, z, z, conv_w, bias)


def _moba_kernel(q_ref, k_ref, vt_ref, bias_ref, o_ref,
                 kaug_ref, kmean_ref, gate_ref, neg_ref, qaug_ref, e_ref,
                 *, case_blocks, chunk_blocks):
    i = pl.program_id(2)
    blk = MOBA_BLOCK
    dh = MOBA_HEAD_DIM
    nblk = k_ref.shape[1] // blk
    group = q_ref.shape[2] // dh
    rows = group * blk
    f32, bf16 = jnp.float32, jnp.bfloat16
    c1 = dh ** -0.5 * LOG2E

    @pl.when(i == 0)
    def _():
        onehot_row = lax.broadcasted_iota(jnp.int32, (blk, dh), 1)
        means = []
        for j in range(nblk):
            kj = k_ref[0, j * blk:(j + 1) * blk, :]
            kaug_ref[j * blk:(j + 1) * blk, 0:dh] = kj
            kaug_ref[j * blk:(j + 1) * blk, dh:2 * dh] = (onehot_row == j).astype(bf16)
            means.append(jnp.mean(kj.astype(f32), axis=0, keepdims=True))
        rest = jnp.concatenate(means, axis=0)
        for part in range(3):
            term = rest.astype(bf16)
            kmean_ref[part * nblk:(part + 1) * nblk, :] = term
            rest = rest - term.astype(f32)
        neg_ref[...] = jnp.zeros(neg_ref.shape, f32)

    q = q_ref[0]
    qs = jnp.concatenate([q[:, g * dh:(g + 1) * dh] for g in range(group)], axis=0)
    qaug_ref[:, 0:dh] = qs

    gate3 = _dot_nt(kmean_ref[...], qs)
    gate = gate3[0:nblk] + gate3[nblk:2 * nblk] + gate3[2 * nblk:3 * nblk]
    gate_ref[...] = gate
    blkid = lax.broadcasted_iota(jnp.int32, gate.shape, 0)
    past = blkid < i
    neg_ref[0:nblk, :] = jnp.zeros((nblk, rows), f32)

    def select(j, carry):
        gj = gate_ref[pl.ds(j, 1), :]
        beats = ((gate > gj) | ((gate == gj) & (blkid < j))) & past
        n_beat = jnp.sum(beats.astype(f32), axis=0, keepdims=True)
        neg_ref[pl.ds(j, 1), :] = jnp.where(n_beat < MOBA_TOPK, 0.0, MASKED)
        return carry

    lax.fori_loop(0, i, select, 0)
    qaug_ref[:, dh:2 * dh] = neg_ref[...].T.astype(bf16)
    qaug = qaug_ref[...]

    def attend(n_keys_blocks):
        m = None
        for c0 in range(0, n_keys_blocks, chunk_blocks):
            cb = min(chunk_blocks, n_keys_blocks - c0)
            s = _dot_nt(kaug_ref[c0 * blk:(c0 + cb) * blk, :], qaug)
            for r in range(cb):
                j = c0 + r
                tile = jnp.where(j > i, 3, jnp.minimum(i - j, 2))
                e = s[r * blk:(r + 1) * blk] * c1 + bias_ref[0, tile]
                e_ref[j * blk:(j + 1) * blk, :] = e
                e_max = jnp.max(e, axis=0, keepdims=True)
                m = e_max if m is None else jnp.maximum(m, e_max)
        acc = jnp.zeros((vt_ref.shape[2], rows), f32)
        for j in range(n_keys_blocks):
            p = jnp.exp2(e_ref[j * blk:(j + 1) * blk, :] - m).astype(bf16)
            acc = acc + _dot(vt_ref[0, 0, :, j * blk:(j + 1) * blk], p)
        out = (acc[0:dh] / acc[dh:dh + 1]).T.astype(o_ref.dtype)
        for g in range(group):
            o_ref[0, :, g * dh:(g + 1) * dh] = out[g * blk:(g + 1) * blk]

    for case in range(nblk // case_blocks):
        @pl.when(i // case_blocks == case)
        def _(case=case):
            attend(case_blocks * (case + 1))


def _moba(qk, vt, table, *, case_blocks=2, chunk_blocks=4):
    b, s, _ = qk.shape
    blk, dh = MOBA_BLOCK, MOBA_HEAD_DIM
    assert s % blk == 0 and dh == V7X_LANES
    nblk = s // blk
    assert nblk % case_blocks == 0 and nblk <= V7X_LANES
    group = MOBA_HEADS // MOBA_KV_HEADS
    rows = group * blk
    qi = np.arange(blk)[None, :]
    kj = np.arange(blk)[:, None]

    def transposed_tile(dist, visible):
        tile = _bias_tile(table, dist, visible) * LOG2E
        tile = tile.reshape(MOBA_KV_HEADS, group, blk, blk)
        return jnp.moveaxis(tile, 1, 2).reshape(MOBA_KV_HEADS, blk, rows)

    everything = np.ones((blk, blk), bool)
    nothing = np.zeros((blk, blk), bool)
    bias = jnp.stack([transposed_tile(qi - kj, qi - kj >= 0),
                      transposed_tile(qi + blk - kj, everything),
                      transposed_tile(qi + 2 * blk - kj, everything),
                      transposed_tile(qi - kj, nothing)], axis=1)
    assert (_rel_bucket(np.arange(blk + 1, 2 * blk * nblk)) == REL_BUCKETS - 1).all()
    k_block0 = MOBA_HEADS
    return pl.pallas_call(
        functools.partial(_moba_kernel, case_blocks=case_blocks, chunk_blocks=chunk_blocks),
        grid=(b, MOBA_KV_HEADS, nblk),
        in_specs=[
            pl.BlockSpec((1, blk, group * dh), lambda bi, h, i: (bi, i, h)),
            pl.BlockSpec((1, s, dh), lambda bi, h, i: (bi, 0, k_block0 + h)),
            pl.BlockSpec((1, 1, vt.shape[2], s), lambda bi, h, i: (bi, h, 0, 0)),
            pl.BlockSpec((1, 4, blk, rows), lambda bi, h, i: (h, 0, 0, 0)),
        ],
        out_specs=pl.BlockSpec((1, blk, group * dh), lambda bi, h, i: (bi, i, h)),
        out_shape=jax.ShapeDtypeStruct((b, s, MOBA_HEADS * dh), jnp.bfloat16),
        scratch_shapes=[
            pltpu.VMEM((s, 2 * dh), jnp.bfloat16),
            pltpu.VMEM((3 * nblk, dh), jnp.bfloat16),
            pltpu.VMEM((nblk, rows), jnp.float32),
            pltpu.VMEM((V7X_LANES, rows), jnp.float32),
            pltpu.VMEM((rows, 2 * dh), jnp.bfloat16),
            pltpu.VMEM((s, rows), jnp.float32),
        ],
        compiler_params=_params("parallel", "parallel", "arbitrary"),
        name="moba",
    )(qk, qk, vt, bias)


def _cross_attention(x, g_ref, wq_ref, kv_ref, wo_ref, a_ref):
    dh = XA_HEAD_DIM
    n = _rms(x, g_ref[...]).astype(jnp.bfloat16)
    q = _dot(n, wq_ref[...]).astype(jnp.bfloat16)
    scale = dh ** -0.5
    v0 = XA_HEADS * dh
    for h in range(XA_HEADS):
        cols = slice(h * dh, (h + 1) * dh)
        s = _dot_nt(q[:, cols], kv_ref[:, cols]) * scale
        m = jnp.max(s, axis=-1, keepdims=True)
        p = jnp.exp(s - m)
        l = jnp.sum(p, axis=-1, keepdims=True)
        out = _dot(p.astype(jnp.bfloat16), kv_ref[:, v0 + h * dh:v0 + (h + 1) * dh]) / l
        a_ref[:, cols] = out.astype(a_ref.dtype)
    return x + _dot(a_ref[...], wo_ref[...])


def _post_mixer_kernel(x_ref, y_ref, wmix_ref, xg_ref, wq_ref, kv_ref, wo_ref, fg_ref, wg_ref,
                       wu_ref, wd_ref, ng_ref, o_ref, a_ref, *, ff_chunk, final_norm):
    x = x_ref[...] + _dot(y_ref[...], wmix_ref[...])
    x = _cross_attention(x, xg_ref, wq_ref, kv_ref, wo_ref, a_ref)
    x = _half_swiglu(x, fg_ref, wg_ref, wu_ref, wd_ref, ff_chunk)
    if final_norm:
        x = _rms(x, ng_ref[...])
    o_ref[...] = x


def _post_mixer(x, y, w_mix, xa_g, w_q, kv, layer, w_o, ffn_g, wg, wu, wd, norm_g, *, final_norm,
                seq, mem_len, tm=512, ff_chunk=256):
    t, d = x.shape
    d_ff = wg.shape[1]
    xa_w = XA_HEADS * XA_HEAD_DIM
    assert seq % tm == 0 and d_ff % ff_chunk == 0
    tiles_per_seq = seq // tm
    row = pl.BlockSpec((tm, d), lambda i: (i, 0))
    return pl.pallas_call(
        functools.partial(_post_mixer_kernel, ff_chunk=ff_chunk, final_norm=final_norm),
        grid=(t // tm,),
        in_specs=[
            row,
            pl.BlockSpec((tm, y.shape[1]), lambda i: (i, 0)),
            _resident(w_mix.shape),
            _resident((1, d)),
            _resident((d, xa_w)),
            pl.BlockSpec((mem_len, 2 * xa_w), lambda i: (i // tiles_per_seq, layer)),
            _resident((xa_w, d)),
            _resident((1, d)),
            _resident((d, d_ff)),
            _resident((d, d_ff)),
            _resident((d_ff, d)),
            _resident((1, d)),
        ],
        out_specs=row,
        out_shape=jax.ShapeDtypeStruct((t, d), jnp.float32),
        scratch_shapes=[pltpu.VMEM((tm, xa_w), jnp.bfloat16)],
        compiler_params=_params("parallel"),
        name="post_mixer",
    )(x, y, w_mix, xa_g, w_q, kv, w_o, ffn_g, wg, wu, wd, norm_g)


def kernel(x, mem, ffn1_norm, ffn1_w_gate, ffn1_w_up, ffn1_w_down, mix_norm, ev_w_in, ev_conv_w,
           ev_sinks, ev_w_out, od_w_in, od_w_out, rel_bias, xa_norm, xa_w_q, xa_w_kv, xa_w_o,
           mem_norm, ffn2_norm, ffn2_w_gate, ffn2_w_up, ffn2_w_down, final_norm):
    b, s, d = x.shape
    depth = ffn1_norm.shape[0]
    mem_len = mem.shape[1]
    bf16 = jnp.bfloat16
    row = lambda g: g.reshape(1, d)

    kv_w = jnp.concatenate([xa_w_kv[l] for l in range(depth)], axis=1).astype(bf16)
    kv = _norm_proj(mem.reshape(b * mem_len, d), row(mem_norm), kv_w, tm=min(512, b * mem_len))

    xt = x.reshape(b * s, d)
    for l in range(depth):
        xt = _ffn(xt, row(ffn1_norm[l]), ffn1_w_gate[l].astype(bf16), ffn1_w_up[l].astype(bf16),
                  ffn1_w_down[l].astype(bf16))
        i = l // 2
        if l % 2 == 0:
            z = _norm_proj(xt, row(mix_norm[l]), ev_w_in[i].astype(bf16))
            y = _even_mixer(z.reshape(b, s, -1), ev_conv_w[i], ev_sinks[i], rel_bias)
            w_mix = ev_w_out[i]
        else:
            qk_w = MOBA_HEADS * MOBA_HEAD_DIM + MOBA_KV_HEADS * MOBA_HEAD_DIM
            w_in = od_w_in[i].astype(bf16)
            qk, vt = _moba_proj(xt, row(mix_norm[l]), w_in[:, :qk_w], w_in[:, qk_w:].T, seq=s)
            y = _moba(qk.reshape(b, s, -1), vt, rel_bias)
            w_mix = od_w_out[i]
        xt = _post_mixer(xt, y.reshape(b * s, -1), w_mix.astype(bf16), row(xa_norm[l]),
                         xa_w_q[l].astype(bf16), kv, l, xa_w_o[l].astype(bf16), row(ffn2_norm[l]),
                         ffn2_w_gate[l].astype(bf16), ffn2_w_up[l].astype(bf16),
                         ffn2_w_down[l].astype(bf16), row(final_norm),
                         final_norm=(l == depth - 1), seq=s, mem_len=mem_len)
    return xt.reshape(b, s, d)
```

```python
import functools
import math

import jax
import jax.numpy as jnp
import numpy as np
from jax import lax
from jax.experimental import pallas as pl
from jax.experimental.pallas import tpu as pltpu

EPS = 1e-6
CONV_WIDTH = 3
CONV_CH = 512
SWA_HEADS = 8
SWA_KV_HEADS = 2
SWA_HEAD_DIM = 64
WINDOW = 128
MOBA_HEADS = 8
MOBA_KV_HEADS = 4
MOBA_HEAD_DIM = 128
MOBA_BLOCK = 256
MOBA_TOPK = 3
MOBA_SUM_ROWS = 16
REL_BUCKETS = 32
REL_MAX_DIST = 128
XA_HEADS = 4
XA_HEAD_DIM = 128

V7X_VMEM_BYTES = 64 * 1024 * 1024
VMEM_LIMIT_BYTES = V7X_VMEM_BYTES * 3 // 4
V7X_LANES = 128
MASKED = -1e30
LOG2E = math.log2(math.e)

_NT = (((1,), (1,)), ((), ()))


def _params(*semantics):
    return pltpu.CompilerParams(dimension_semantics=semantics, vmem_limit_bytes=VMEM_LIMIT_BYTES)


def _resident(shape):
    zeros = (0,) * len(shape)
    return pl.BlockSpec(shape, lambda *_: zeros, pipeline_mode=pl.Buffered(1))


def _rms(x, g):
    return x * lax.rsqrt(jnp.mean(x * x, axis=-1, keepdims=True) + EPS) * g


def _dot(a, b):
    return jnp.dot(a, b, preferred_element_type=jnp.float32)


def _dot_nt(a, b):
    return lax.dot_general(a, b, _NT, preferred_element_type=jnp.float32)


def _half_swiglu(x, g_ref, wg_ref, wu_ref, wd_ref, ff_chunk):
    n = _rms(x, g_ref[...]).astype(jnp.bfloat16)
    d_ff = wg_ref.shape[1]
    acc = jnp.zeros(x.shape, jnp.float32)
    for c in range(d_ff // ff_chunk):
        cols = slice(c * ff_chunk, (c + 1) * ff_chunk)
        gate = _dot(n, wg_ref[:, cols])
        up = _dot(n, wu_ref[:, cols])
        h = (gate * jax.nn.sigmoid(gate) * up).astype(jnp.bfloat16)
        acc = acc + _dot(h, wd_ref[cols, :])
    return x + 0.5 * acc


def _ffn_kernel(x_ref, g_ref, wg_ref, wu_ref, wd_ref, o_ref, *, ff_chunk):
    o_ref[...] = _half_swiglu(x_ref[...], g_ref, wg_ref, wu_ref, wd_ref, ff_chunk)


def _ffn(x, g, wg, wu, wd, *, tm=512, ff_chunk=256):
    t, d = x.shape
    d_ff = wg.shape[1]
    assert t % tm == 0 and d_ff % ff_chunk == 0
    row = pl.BlockSpec((tm, d), lambda i: (i, 0))
    return pl.pallas_call(
        functools.partial(_ffn_kernel, ff_chunk=ff_chunk),
        grid=(t // tm,),
        in_specs=[row, _resident((1, d)), _resident((d, d_ff)), _resident((d, d_ff)),
                  _resident((d_ff, d))],
        out_specs=row,
        out_shape=jax.ShapeDtypeStruct((t, d), jnp.float32),
        compiler_params=_params("parallel"),
        name="ffn",
    )(x, g, wg, wu, wd)


def _norm_proj_kernel(x_ref, g_ref, w_ref, o_ref):
    n = _rms(x_ref[...], g_ref[...]).astype(jnp.bfloat16)
    o_ref[...] = _dot(n, w_ref[...]).astype(o_ref.dtype)


def _norm_proj(x, g, w, *, tm=512):
    t, d = x.shape
    n_out = w.shape[1]
    assert t % tm == 0
    return pl.pallas_call(
        _norm_proj_kernel,
        grid=(t // tm,),
        in_specs=[pl.BlockSpec((tm, d), lambda i: (i, 0)), _resident((1, d)), _resident((d, n_out))],
        out_specs=pl.BlockSpec((tm, n_out), lambda i: (i, 0)),
        out_shape=jax.ShapeDtypeStruct((t, n_out), jnp.bfloat16),
        compiler_params=_params("parallel"),
        name="norm_proj",
    )(x, g, w)


def _moba_proj_kernel(x_ref, g_ref, wqk_ref, wvt_ref, qk_ref, vt_ref):
    n = _rms(x_ref[...], g_ref[...]).astype(jnp.bfloat16)
    qk_ref[...] = _dot(n, wqk_ref[...]).astype(qk_ref.dtype)
    vt = _dot_nt(wvt_ref[...], n).astype(vt_ref.dtype)
    dh = MOBA_HEAD_DIM
    for h in range(MOBA_KV_HEADS):
        vt_ref[0, h, 0:dh] = vt[h * dh:(h + 1) * dh, :]
        vt_ref[0, h, dh:dh + MOBA_SUM_ROWS] = jnp.ones((MOBA_SUM_ROWS, vt.shape[1]), vt_ref.dtype)


def _moba_proj(x, g, w_qk, w_vt, *, seq, tm=512):
    t, d = x.shape
    n_qk = w_qk.shape[1]
    dh = MOBA_HEAD_DIM + MOBA_SUM_ROWS
    assert seq % tm == 0
    tiles_per_seq = seq // tm
    return pl.pallas_call(
        _moba_proj_kernel,
        grid=(t // tm,),
        in_specs=[pl.BlockSpec((tm, d), lambda i: (i, 0)), _resident((1, d)), _resident(w_qk.shape),
                  _resident(w_vt.shape)],
        out_specs=[
            pl.BlockSpec((tm, n_qk), lambda i: (i, 0)),
            pl.BlockSpec((1, MOBA_KV_HEADS, dh, tm),
                         lambda i: (i // tiles_per_seq, 0, 0, i % tiles_per_seq)),
        ],
        out_shape=[
            jax.ShapeDtypeStruct((t, n_qk), jnp.bfloat16),
            jax.ShapeDtypeStruct((t // seq, MOBA_KV_HEADS, dh, seq), jnp.bfloat16),
        ],
        compiler_params=_params("parallel"),
        name="moba_proj",
    )(x, g, w_qk, w_vt)


def _rel_bucket(dist):
    n = np.maximum(dist, 0)
    max_exact = REL_BUCKETS // 2
    nf = np.maximum(n, 1).astype(np.float32)
    scaled = (np.log(nf / np.float32(max_exact)) / np.float32(math.log(REL_MAX_DIST / max_exact))
              * np.float32(REL_BUCKETS - max_exact))
    large = np.minimum(max_exact + scaled.astype(np.int32), REL_BUCKETS - 1)
    return np.where(n < max_exact, n, large).astype(np.int32)


def _bias_tile(table, dist, visible):
    bucket = _rel_bucket(dist)
    bias = jnp.full((table.shape[1],) + dist.shape, MASKED, jnp.float32)
    for b in np.unique(bucket[visible]):
        bias = jnp.where((bucket == b) & visible, table[b][:, None, None], bias)
    return bias


def _even_mixer_kernel(sinks_ref, z_ref, kvh_ref, ch_ref, uh_ref, convw_ref, bias_ref, y_ref,
                       *, halo_rows):
    i = pl.program_id(1)
    ts = z_ref.shape[1]
    c1, c2, c3 = CONV_CH, 2 * CONV_CH, 3 * CONV_CH
    q0 = c3
    k0 = q0 + SWA_HEADS * SWA_HEAD_DIM
    v0 = k0 + SWA_KV_HEADS * SWA_HEAD_DIM
    group = SWA_HEADS // SWA_KV_HEADS
    dh = SWA_HEAD_DIM
    first = i == 0

    f32 = jnp.float32
    v = z_ref[0, :, c1:c2].astype(f32) * z_ref[0, :, c2:c3].astype(f32)
    vh = ch_ref[0].astype(f32) * uh_ref[0].astype(f32)
    vh = jnp.where(first, 0.0, vh)
    row = lax.broadcasted_iota(jnp.int32, v.shape, 0)
    h1 = vh[halo_rows - 1:halo_rows, :]
    h2 = vh[halo_rows - 2:halo_rows - 1, :]
    v1 = jnp.where(row == 0, h1, pltpu.roll(v, 1, 0))
    v2 = jnp.where(row == 0, h2, jnp.where(row == 1, h1, pltpu.roll(v, 2, 0)))
    conv = v2 * convw_ref[0:1, :] + v1 * convw_ref[1:2, :] + v * convw_ref[2:3, :]
    y_ref[0, :, 0:CONV_CH] = (z_ref[0, :, 0:c1].astype(f32) * conv).astype(y_ref.dtype)

    scale = dh ** -0.5
    col = lax.broadcasted_iota(jnp.int32, (WINDOW, 2 * WINDOW), 1)
    for qb in range(ts // WINDOW):
        rows = slice(qb * WINDOW, (qb + 1) * WINDOW)
        for kh in range(SWA_KV_HEADS):
            kcol = slice(k0 + kh * dh, k0 + (kh + 1) * dh)
            vcol = slice(v0 + kh * dh, v0 + (kh + 1) * dh)
            if qb == 0:
                k_prev = kvh_ref[0, :, kh * dh:(kh + 1) * dh]
                v_prev = kvh_ref[0, :, (SWA_KV_HEADS + kh) * dh:(SWA_KV_HEADS + kh + 1) * dh]
            else:
                prev_rows = slice((qb - 1) * WINDOW, qb * WINDOW)
                k_prev = z_ref[0, prev_rows, kcol]
                v_prev = z_ref[0, prev_rows, vcol]
            kc = jnp.concatenate([k_prev, z_ref[0, rows, kcol]], axis=0)
            vc = jnp.concatenate([v_prev, z_ref[0, rows, vcol]], axis=0)
            qs = jnp.concatenate(
                [z_ref[0, rows, q0 + (kh * group + g) * dh:q0 + (kh * group + g + 1) * dh]
                 for g in range(group)], axis=0)
            logits = _dot_nt(qs, kc) * scale + bias_ref[kh]
            for g in range(group):
                head = kh * group + g
                lg = logits[g * WINDOW:(g + 1) * WINDOW]
                if qb == 0:
                    lg = jnp.where(first & (col < WINDOW), MASKED, lg)
                sink = sinks_ref[head]
                m = jnp.maximum(jnp.max(lg, axis=-1, keepdims=True), sink)
                p = jnp.exp(lg - m)
                denom = jnp.sum(p, axis=-1, keepdims=True) + jnp.exp(sink - m)
                out = _dot(p.astype(jnp.bfloat16), vc) / denom
                y_ref[0, rows, CONV_CH + head * dh:CONV_CH + (head + 1) * dh] = out.astype(y_ref.dtype)


def _even_mixer(z, conv_w, sinks, table, *, ts=512, halo_rows=16):
    b, s, n_in = z.shape
    n_out = CONV_CH + SWA_HEADS * SWA_HEAD_DIM
    assert s % ts == 0 and ts % WINDOW == 0
    dist = np.arange(WINDOW)[:, None] + WINDOW - np.arange(2 * WINDOW)[None, :]
    bias = _bias_tile(table, dist, (dist >= 0) & (dist < WINDOW))
    bias = bias.reshape(SWA_KV_HEADS, (SWA_HEADS // SWA_KV_HEADS) * WINDOW, 2 * WINDOW)
    kv_w = 2 * SWA_KV_HEADS * SWA_HEAD_DIM
    kv_block = (3 * CONV_CH + SWA_HEADS * SWA_HEAD_DIM) // kv_w
    w_per_tile = ts // WINDOW
    h_per_tile = ts // halo_rows
    return pl.pallas_call(
        functools.partial(_even_mixer_kernel, halo_rows=halo_rows),
        grid=(b, s // ts),
        in_specs=[
            pl.BlockSpec(memory_space=pltpu.SMEM),
            pl.BlockSpec((1, ts, n_in), lambda bi, i: (bi, i, 0)),
            pl.BlockSpec((1, WINDOW, kv_w), lambda bi, i: (bi, jnp.maximum(i * w_per_tile - 1, 0), kv_block)),
            pl.BlockSpec((1, halo_rows, CONV_CH), lambda bi, i: (bi, jnp.maximum(i * h_per_tile - 1, 0), 1)),
            pl.BlockSpec((1, halo_rows, CONV_CH), lambda bi, i: (bi, jnp.maximum(i * h_per_tile - 1, 0), 2)),
            _resident(conv_w.shape),
            _resident(bias.shape),
        ],
        out_specs=pl.BlockSpec((1, ts, n_out), lambda bi, i: (bi, i, 0)),
        out_shape=jax.ShapeDtypeStruct((b, s, n_out), jnp.bfloat16),
        compiler_params=_params("parallel", "parallel"),
        name="even_mixer",
    )(sinks, z, z, z, z, conv_w, bias)


def _moba_kernel(q_ref, k_ref, vt_ref, bias_ref, o_ref,
                 kaug_ref, kmean_ref, gate_ref, neg_ref, qaug_ref, e_ref,
                 *, q_blocks, chunk_blocks):
    step = pl.program_id(2)
    blk = MOBA_BLOCK
    dh = MOBA_HEAD_DIM
    nblk = k_ref.shape[1] // blk
    group = q_ref.shape[2] // dh
    rows = q_blocks * group * blk
    f32, bf16 = jnp.float32, jnp.bfloat16
    c1 = dh ** -0.5 * LOG2E

    @pl.when(step == 0)
    def _():
        onehot_row = lax.broadcasted_iota(jnp.int32, (blk, dh), 1)
        means = []
        for j in range(nblk):
            kj = k_ref[0, j * blk:(j + 1) * blk, :]
            kaug_ref[j * blk:(j + 1) * blk, 0:dh] = kj
            kaug_ref[j * blk:(j + 1) * blk, dh:2 * dh] = (onehot_row == j).astype(bf16)
            means.append(jnp.mean(kj.astype(f32), axis=0, keepdims=True))
        rest = jnp.concatenate(means, axis=0)
        for part in range(3):
            term = rest.astype(bf16)
            kmean_ref[part * nblk:(part + 1) * nblk, :] = term
            rest = rest - term.astype(f32)
        neg_ref[...] = jnp.zeros(neg_ref.shape, f32)

    q = q_ref[0]
    qs = jnp.concatenate([q[sb * blk:(sb + 1) * blk, g * dh:(g + 1) * dh]
                          for sb in range(q_blocks) for g in range(group)], axis=0)
    qaug_ref[:, 0:dh] = qs
    lane = lax.broadcasted_iota(jnp.int32, (1, rows), 1)
    own = step * q_blocks + lane // (group * blk)

    gate3 = _dot_nt(kmean_ref[...], qs)
    gate = gate3[0:nblk] + gate3[nblk:2 * nblk] + gate3[2 * nblk:3 * nblk]
    gate_ref[...] = gate
    blkid = lax.broadcasted_iota(jnp.int32, gate.shape, 0)
    past = blkid < own
    neg_ref[0:nblk, :] = jnp.zeros((nblk, rows), f32)

    def select(j, carry):
        gj = gate_ref[pl.ds(j, 1), :]
        beats = ((gate > gj) | ((gate == gj) & (blkid < j))) & past
        n_beat = jnp.sum(beats.astype(f32), axis=0, keepdims=True)
        keep = (n_beat < MOBA_TOPK) | (j >= own)
        neg_ref[pl.ds(j, 1), :] = jnp.where(keep, 0.0, MASKED)
        return carry

    lax.fori_loop(0, (step + 1) * q_blocks - 1, select, 0)
    qaug_ref[:, dh:2 * dh] = neg_ref[...].T.astype(bf16)
    qaug = qaug_ref[...]

    def attend(first_q_block):
        n_keys_blocks = first_q_block + q_blocks

        def tile_of(j, qb):
            return 3 if j > qb else min(qb - j, 2)

        m = None
        for c0 in range(0, n_keys_blocks, chunk_blocks):
            cb = min(chunk_blocks, n_keys_blocks - c0)
            s = _dot_nt(kaug_ref[c0 * blk:(c0 + cb) * blk, :], qaug)
            for r in range(cb):
                j = c0 + r
                bias = jnp.concatenate([bias_ref[0, tile_of(j, first_q_block + sb)]
                                        for sb in range(q_blocks)], axis=1)
                e = s[r * blk:(r + 1) * blk] * c1 + bias
                e_ref[j * blk:(j + 1) * blk, :] = e
                e_max = jnp.max(e, axis=0, keepdims=True)
                m = e_max if m is None else jnp.maximum(m, e_max)
        acc = jnp.zeros((vt_ref.shape[2], rows), f32)
        for j in range(n_keys_blocks):
            p = jnp.exp2(e_ref[j * blk:(j + 1) * blk, :] - m).astype(bf16)
            acc = acc + _dot(vt_ref[0, 0, :, j * blk:(j + 1) * blk], p)
        out = (acc[0:dh] / acc[dh:dh + 1]).T.astype(o_ref.dtype)
        for sb in range(q_blocks):
            for g in range(group):
                r0 = (sb * group + g) * blk
                o_ref[0, sb * blk:(sb + 1) * blk, g * dh:(g + 1) * dh] = out[r0:r0 + blk]

    for case in range(nblk // q_blocks):
        @pl.when(step == case)
        def _(case=case):
            attend(case * q_blocks)


def _moba(qk, vt, table, *, q_blocks=2, chunk_blocks=4):
    b, s, _ = qk.shape
    blk, dh = MOBA_BLOCK, MOBA_HEAD_DIM
    assert s % blk == 0 and dh == V7X_LANES
    nblk = s // blk
    assert nblk % q_blocks == 0 and nblk <= V7X_LANES
    group = MOBA_HEADS // MOBA_KV_HEADS
    rows = group * blk
    qi = np.arange(blk)[None, :]
    kj = np.arange(blk)[:, None]

    def transposed_tile(dist, visible):
        tile = _bias_tile(table, dist, visible) * LOG2E
        tile = tile.reshape(MOBA_KV_HEADS, group, blk, blk)
        return jnp.moveaxis(tile, 1, 2).reshape(MOBA_KV_HEADS, blk, rows)

    everything = np.ones((blk, blk), bool)
    nothing = np.zeros((blk, blk), bool)
    bias = jnp.stack([transposed_tile(qi - kj, qi - kj >= 0),
                      transposed_tile(qi + blk - kj, everything),
                      transposed_tile(qi + 2 * blk - kj, everything),
                      transposed_tile(qi - kj, nothing)], axis=1)
    assert (_rel_bucket(np.arange(blk + 1, 2 * blk * nblk)) == REL_BUCKETS - 1).all()
    k_block0 = MOBA_HEADS
    return pl.pallas_call(
        functools.partial(_moba_kernel, q_blocks=q_blocks, chunk_blocks=chunk_blocks),
        grid=(b, MOBA_KV_HEADS, nblk // q_blocks),
        in_specs=[
            pl.BlockSpec((1, q_blocks * blk, group * dh), lambda bi, h, i: (bi, i, h)),
            pl.BlockSpec((1, s, dh), lambda bi, h, i: (bi, 0, k_block0 + h)),
            pl.BlockSpec((1, 1, vt.shape[2], s), lambda bi, h, i: (bi, h, 0, 0)),
            pl.BlockSpec((1, 4, blk, rows), lambda bi, h, i: (h, 0, 0, 0)),
        ],
        out_specs=pl.BlockSpec((1, q_blocks * blk, group * dh), lambda bi, h, i: (bi, i, h)),
        out_shape=jax.ShapeDtypeStruct((b, s, MOBA_HEADS * dh), jnp.bfloat16),
        scratch_shapes=[
            pltpu.VMEM((s, 2 * dh), jnp.bfloat16),
            pltpu.VMEM((3 * nblk, dh), jnp.bfloat16),
            pltpu.VMEM((nblk, q_blocks * rows), jnp.float32),
            pltpu.VMEM((V7X_LANES, q_blocks * rows), jnp.float32),
            pltpu.VMEM((q_blocks * rows, 2 * dh), jnp.bfloat16),
            pltpu.VMEM((s, q_blocks * rows), jnp.float32),
        ],
        compiler_params=_params("parallel", "parallel", "arbitrary"),
        name="moba",
    )(qk, qk, vt, bias)


def _cross_attention(x, g_ref, wq_ref, kv_ref, wo_ref, a_ref):
    dh = XA_HEAD_DIM
    n = _rms(x, g_ref[...]).astype(jnp.bfloat16)
    q = _dot(n, wq_ref[...]).astype(jnp.bfloat16)
    scale = dh ** -0.5
    v0 = XA_HEADS * dh
    for h in range(XA_HEADS):
        cols = slice(h * dh, (h + 1) * dh)
        s = _dot_nt(q[:, cols], kv_ref[:, cols]) * scale
        m = jnp.max(s, axis=-1, keepdims=True)
        p = jnp.exp(s - m)
        l = jnp.sum(p, axis=-1, keepdims=True)
        out = _dot(p.astype(jnp.bfloat16), kv_ref[:, v0 + h * dh:v0 + (h + 1) * dh]) / l
        a_ref[:, cols] = out.astype(a_ref.dtype)
    return x + _dot(a_ref[...], wo_ref[...])


def _post_mixer_kernel(x_ref, y_ref, wmix_ref, xg_ref, wq_ref, kv_ref, wo_ref, fg_ref, wg_ref,
                       wu_ref, wd_ref, ng_ref, o_ref, a_ref, *, ff_chunk, final_norm):
    x = x_ref[...] + _dot(y_ref[...], wmix_ref[...])
    x = _cross_attention(x, xg_ref, wq_ref, kv_ref, wo_ref, a_ref)
    x = _half_swiglu(x, fg_ref, wg_ref, wu_ref, wd_ref, ff_chunk)
    if final_norm:
        x = _rms(x, ng_ref[...])
    o_ref[...] = x


def _post_mixer(x, y, w_mix, xa_g, w_q, kv, layer, w_o, ffn_g, wg, wu, wd, norm_g, *, final_norm,
                seq, mem_len, tm=512, ff_chunk=256):
    t, d = x.shape
    d_ff = wg.shape[1]
    xa_w = XA_HEADS * XA_HEAD_DIM
    assert seq % tm == 0 and d_ff % ff_chunk == 0
    tiles_per_seq = seq // tm
    row = pl.BlockSpec((tm, d), lambda i: (i, 0))
    return pl.pallas_call(
        functools.partial(_post_mixer_kernel, ff_chunk=ff_chunk, final_norm=final_norm),
        grid=(t // tm,),
        in_specs=[
            row,
            pl.BlockSpec((tm, y.shape[1]), lambda i: (i, 0)),
            _resident(w_mix.shape),
            _resident((1, d)),
            _resident((d, xa_w)),
            pl.BlockSpec((mem_len, 2 * xa_w), lambda i: (i // tiles_per_seq, layer)),
            _resident((xa_w, d)),
            _resident((1, d)),
            _resident((d, d_ff)),
            _resident((d, d_ff)),
            _resident((d_ff, d)),
            _resident((1, d)),
        ],
        out_specs=row,
        out_shape=jax.ShapeDtypeStruct((t, d), jnp.float32),
        scratch_shapes=[pltpu.VMEM((tm, xa_w), jnp.bfloat16)],
        compiler_params=_params("parallel"),
        name="post_mixer",
    )(x, y, w_mix, xa_g, w_q, kv, w_o, ffn_g, wg, wu, wd, norm_g)


def kernel(x, mem, ffn1_norm, ffn1_w_gate, ffn1_w_up, ffn1_w_down, mix_norm, ev_w_in, ev_conv_w,
           ev_sinks, ev_w_out, od_w_in, od_w_out, rel_bias, xa_norm, xa_w_q, xa_w_kv, xa_w_o,
           mem_norm, ffn2_norm, ffn2_w_gate, ffn2_w_up, ffn2_w_down, final_norm):
    b, s, d = x.shape
    depth = ffn1_norm.shape[0]
    mem_len = mem.shape[1]
    bf16 = jnp.bfloat16
    row = lambda g: g.reshape(1, d)

    kv_w = jnp.concatenate([xa_w_kv[l] for l in range(depth)], axis=1).astype(bf16)
    kv = _norm_proj(mem.reshape(b * mem_len, d), row(mem_norm), kv_w, tm=min(512, b * mem_len))

    xt = x.reshape(b * s, d)
    for l in range(depth):
        xt = _ffn(xt, row(ffn1_norm[l]), ffn1_w_gate[l].astype(bf16), ffn1_w_up[l].astype(bf16),
                  ffn1_w_down[l].astype(bf16))
        i = l // 2
        if l % 2 == 0:
            z = _norm_proj(xt, row(mix_norm[l]), ev_w_in[i].astype(bf16))
            y = _even_mixer(z.reshape(b, s, -1), ev_conv_w[i], ev_sinks[i], rel_bias)
            w_mix = ev_w_out[i]
        else:
            qk_w = MOBA_HEADS * MOBA_HEAD_DIM + MOBA_KV_HEADS * MOBA_HEAD_DIM
            w_in = od_w_in[i].astype(bf16)
            qk, vt = _moba_proj(xt, row(mix_norm[l]), w_in[:, :qk_w], w_in[:, qk_w:].T, seq=s)
            y = _moba(qk.reshape(b, s, -1), vt, rel_bias)
            w_mix = od_w_out[i]
        xt = _post_mixer(xt, y.reshape(b * s, -1), w_mix.astype(bf16), row(xa_norm[l]),
                         xa_w_q[l].astype(bf16), kv, l, xa_w_o[l].astype(bf16), row(ffn2_norm[l]),
                         ffn2_w_gate[l].astype(bf16), ffn2_w_up[l].astype(bf16),
                         ffn2_w_down[l].astype(bf16), row(final_norm),
                         final_norm=(l == depth - 1), seq=s, mem_len=mem_len)
    return xt.reshape(b, s, d)
```

```python
import functools
import math

import jax
import jax.numpy as jnp
import numpy as np
from jax import lax
from jax.experimental import pallas as pl
from jax.experimental.pallas import tpu as pltpu

EPS = 1e-6
CONV_WIDTH = 3
CONV_CH = 512
SWA_HEADS = 8
SWA_KV_HEADS = 2
SWA_HEAD_DIM = 64
WINDOW = 128
MOBA_HEADS = 8
MOBA_KV_HEADS = 4
MOBA_HEAD_DIM = 128
MOBA_BLOCK = 256
MOBA_TOPK = 3
MOBA_SUM_ROWS = 16
SWA_SUM_ROWS = 64
REL_BUCKETS = 32
REL_MAX_DIST = 128
XA_HEADS = 4
XA_HEAD_DIM = 128

V7X_VMEM_BYTES = 64 * 1024 * 1024
VMEM_LIMIT_BYTES = V7X_VMEM_BYTES * 3 // 4
V7X_LANES = 128
MASKED = -1e30
LOG2E = math.log2(math.e)

_NT = (((1,), (1,)), ((), ()))


def _params(*semantics):
    return pltpu.CompilerParams(dimension_semantics=semantics, vmem_limit_bytes=VMEM_LIMIT_BYTES)


def _resident(shape):
    zeros = (0,) * len(shape)
    return pl.BlockSpec(shape, lambda *_: zeros, pipeline_mode=pl.Buffered(1))


def _rms(x, g):
    return x * lax.rsqrt(jnp.mean(x * x, axis=-1, keepdims=True) + EPS) * g


def _dot(a, b):
    return jnp.dot(a, b, preferred_element_type=jnp.float32)


def _dot_nt(a, b):
    return lax.dot_general(a, b, _NT, preferred_element_type=jnp.float32)


def _half_swiglu(x, g_ref, wg_ref, wu_ref, wd_ref, ff_chunk):
    n = _rms(x, g_ref[...]).astype(jnp.bfloat16)
    d_ff = wg_ref.shape[1]
    acc = jnp.zeros(x.shape, jnp.float32)
    for c in range(d_ff // ff_chunk):
        cols = slice(c * ff_chunk, (c + 1) * ff_chunk)
        gate = _dot(n, wg_ref[:, cols])
        up = _dot(n, wu_ref[:, cols])
        h = (gate * jax.nn.sigmoid(gate) * up).astype(jnp.bfloat16)
        acc = acc + _dot(h, wd_ref[cols, :])
    return x + 0.5 * acc


def _ffn_kernel(x_ref, g_ref, wg_ref, wu_ref, wd_ref, o_ref, *, ff_chunk):
    o_ref[...] = _half_swiglu(x_ref[...], g_ref, wg_ref, wu_ref, wd_ref, ff_chunk)


def _ffn(x, g, wg, wu, wd, *, tm=512, ff_chunk=256):
    t, d = x.shape
    d_ff = wg.shape[1]
    assert t % tm == 0 and d_ff % ff_chunk == 0
    row = pl.BlockSpec((tm, d), lambda i: (i, 0))
    return pl.pallas_call(
        functools.partial(_ffn_kernel, ff_chunk=ff_chunk),
        grid=(t // tm,),
        in_specs=[row, _resident((1, d)), _resident((d, d_ff)), _resident((d, d_ff)),
                  _resident((d_ff, d))],
        out_specs=row,
        out_shape=jax.ShapeDtypeStruct((t, d), jnp.float32),
        compiler_params=_params("parallel"),
        name="ffn",
    )(x, g, wg, wu, wd)


def _norm_proj_kernel(x_ref, g_ref, w_ref, o_ref):
    n = _rms(x_ref[...], g_ref[...]).astype(jnp.bfloat16)
    o_ref[...] = _dot(n, w_ref[...]).astype(o_ref.dtype)


def _norm_proj(x, g, w, *, tm=512):
    t, d = x.shape
    n_out = w.shape[1]
    assert t % tm == 0
    return pl.pallas_call(
        _norm_proj_kernel,
        grid=(t // tm,),
        in_specs=[pl.BlockSpec((tm, d), lambda i: (i, 0)), _resident((1, d)), _resident((d, n_out))],
        out_specs=pl.BlockSpec((tm, n_out), lambda i: (i, 0)),
        out_shape=jax.ShapeDtypeStruct((t, n_out), jnp.bfloat16),
        compiler_params=_params("parallel"),
        name="norm_proj",
    )(x, g, w)


def _mixer_proj_kernel(x_ref, g_ref, w_ref, wvt_ref, z_ref, vt_ref, *, head_dim):
    n = _rms(x_ref[...], g_ref[...]).astype(jnp.bfloat16)
    z_ref[...] = _dot(n, w_ref[...]).astype(z_ref.dtype)
    vt = _dot_nt(wvt_ref[...], n).astype(vt_ref.dtype)
    sum_rows = vt_ref.shape[2] - head_dim
    for h in range(vt_ref.shape[1]):
        vt_ref[0, h, 0:head_dim] = vt[h * head_dim:(h + 1) * head_dim, :]
        vt_ref[0, h, head_dim:] = jnp.ones((sum_rows, vt.shape[1]), vt_ref.dtype)


def _mixer_proj(x, g, w, w_vt, *, head_dim, sum_rows, seq, tm=512):
    t, d = x.shape
    n_out = w.shape[1]
    kv_heads = w_vt.shape[0] // head_dim
    rows = head_dim + sum_rows
    assert seq % tm == 0
    tiles_per_seq = seq // tm
    return pl.pallas_call(
        functools.partial(_mixer_proj_kernel, head_dim=head_dim),
        grid=(t // tm,),
        in_specs=[pl.BlockSpec((tm, d), lambda i: (i, 0)), _resident((1, d)), _resident(w.shape),
                  _resident(w_vt.shape)],
        out_specs=[
            pl.BlockSpec((tm, n_out), lambda i: (i, 0)),
            pl.BlockSpec((1, kv_heads, rows, tm),
                         lambda i: (i // tiles_per_seq, 0, 0, i % tiles_per_seq)),
        ],
        out_shape=[
            jax.ShapeDtypeStruct((t, n_out), jnp.bfloat16),
            jax.ShapeDtypeStruct((t // seq, kv_heads, rows, seq), jnp.bfloat16),
        ],
        compiler_params=_params("parallel"),
        name="mixer_proj",
    )(x, g, w, w_vt)


def _rel_bucket(dist):
    n = np.maximum(dist, 0)
    max_exact = REL_BUCKETS // 2
    nf = np.maximum(n, 1).astype(np.float32)
    scaled = (np.log(nf / np.float32(max_exact)) / np.float32(math.log(REL_MAX_DIST / max_exact))
              * np.float32(REL_BUCKETS - max_exact))
    large = np.minimum(max_exact + scaled.astype(np.int32), REL_BUCKETS - 1)
    return np.where(n < max_exact, n, large).astype(np.int32)


def _bias_tile(table, dist, visible):
    bucket = _rel_bucket(dist)
    bias = jnp.full((table.shape[1],) + dist.shape, MASKED, jnp.float32)
    for b in np.unique(bucket[visible]):
        bias = jnp.where((bucket == b) & visible, table[b][:, None, None], bias)
    return bias


def _even_mixer_kernel(sinks_ref, z_ref, kh_ref, vt_ref, vth_ref, ch_ref, uh_ref, convw_ref,
                       bias_ref, y_ref, *, halo_rows):
    i = pl.program_id(1)
    ts = z_ref.shape[1]
    c1, c2, c3 = CONV_CH, 2 * CONV_CH, 3 * CONV_CH
    q0 = c3
    k0 = q0 + SWA_HEADS * SWA_HEAD_DIM
    group = SWA_HEADS // SWA_KV_HEADS
    dh = SWA_HEAD_DIM
    first = i == 0

    f32 = jnp.float32
    v = z_ref[0, :, c1:c2].astype(f32) * z_ref[0, :, c2:c3].astype(f32)
    vh = ch_ref[0].astype(f32) * uh_ref[0].astype(f32)
    vh = jnp.where(first, 0.0, vh)
    row = lax.broadcasted_iota(jnp.int32, v.shape, 0)
    h1 = vh[halo_rows - 1:halo_rows, :]
    h2 = vh[halo_rows - 2:halo_rows - 1, :]
    v1 = jnp.where(row == 0, h1, pltpu.roll(v, 1, 0))
    v2 = jnp.where(row == 0, h2, jnp.where(row == 1, h1, pltpu.roll(v, 2, 0)))
    conv = v2 * convw_ref[0:1, :] + v1 * convw_ref[1:2, :] + v * convw_ref[2:3, :]
    y_ref[0, :, 0:CONV_CH] = (z_ref[0, :, 0:c1].astype(f32) * conv).astype(y_ref.dtype)

    c_s = dh ** -0.5 * LOG2E
    lanes = group * WINDOW
    lane = lax.broadcasted_iota(jnp.int32, (1, lanes), 1)
    for kh in range(SWA_KV_HEADS):
        sink = jnp.zeros((1, lanes), f32)
        for g in range(group):
            sink = jnp.where(lane // WINDOW == g, sinks_ref[kh * group + g] * LOG2E, sink)
        kcol = slice(k0 + kh * dh, k0 + (kh + 1) * dh)
        for qb in range(ts // WINDOW):
            rows = slice(qb * WINDOW, (qb + 1) * WINDOW)
            if qb == 0:
                kc = jnp.concatenate([kh_ref[0, :, kh * dh:(kh + 1) * dh], z_ref[0, rows, kcol]],
                                     axis=0)
                vt = jnp.concatenate([vth_ref[0, kh], vt_ref[0, kh, :, 0:WINDOW]], axis=1)
                bias = bias_ref[kh, jnp.where(first, 1, 0)]
            else:
                kc = z_ref[0, (qb - 1) * WINDOW:(qb + 1) * WINDOW, kcol]
                vt = vt_ref[0, kh, :, (qb - 1) * WINDOW:(qb + 1) * WINDOW]
                bias = bias_ref[kh, 0]
            qs = jnp.concatenate(
                [z_ref[0, rows, q0 + (kh * group + g) * dh:q0 + (kh * group + g + 1) * dh]
                 for g in range(group)], axis=0)
            e = _dot_nt(kc, qs) * c_s + bias
            m = jnp.maximum(jnp.max(e, axis=0, keepdims=True), sink)
            p = jnp.exp2(e - m).astype(jnp.bfloat16)
            acc = _dot(vt, p)
            denom = acc[dh:dh + 1] + jnp.exp2(sink - m)
            out = (acc / denom).T
            for g in range(group):
                head = kh * group + g
                y_ref[0, rows, CONV_CH + head * dh:CONV_CH + (head + 1) * dh] = (
                    out[g * WINDOW:(g + 1) * WINDOW, 0:dh].astype(y_ref.dtype))


def _even_mixer(z, vt, conv_w, sinks, table, *, ts=512, halo_rows=16):
    b, s, n_in = z.shape
    n_out = CONV_CH + SWA_HEADS * SWA_HEAD_DIM
    group = SWA_HEADS // SWA_KV_HEADS
    assert s % ts == 0 and ts % WINDOW == 0 and vt.shape[2] == V7X_LANES
    kk = np.arange(2 * WINDOW)[:, None]
    dist = np.arange(WINDOW)[None, :] + WINDOW - kk
    band = (dist >= 0) & (dist < WINDOW)

    def transposed_tile(visible):
        tile = _bias_tile(table, dist, visible) * LOG2E
        tile = tile.reshape(SWA_KV_HEADS, group, 2 * WINDOW, WINDOW)
        return jnp.moveaxis(tile, 1, 2).reshape(SWA_KV_HEADS, 2 * WINDOW, group * WINDOW)

    bias = jnp.stack([transposed_tile(band), transposed_tile(band & (kk >= WINDOW))], axis=1)
    k_w = SWA_KV_HEADS * SWA_HEAD_DIM
    k_block = (3 * CONV_CH + SWA_HEADS * SWA_HEAD_DIM) // k_w
    w_per_tile = ts // WINDOW
    h_per_tile = ts // halo_rows
    prev_window = lambda i: jnp.maximum(i * w_per_tile - 1, 0)
    prev_rows = lambda i: jnp.maximum(i * h_per_tile - 1, 0)
    return pl.pallas_call(
        functools.partial(_even_mixer_kernel, halo_rows=halo_rows),
        grid=(b, s // ts),
        in_specs=[
            pl.BlockSpec(memory_space=pltpu.SMEM),
            pl.BlockSpec((1, ts, n_in), lambda bi, i: (bi, i, 0)),
            pl.BlockSpec((1, WINDOW, k_w), lambda bi, i: (bi, prev_window(i), k_block)),
            pl.BlockSpec((1,) + vt.shape[1:3] + (ts,), lambda bi, i: (bi, 0, 0, i)),
            pl.BlockSpec((1,) + vt.shape[1:3] + (WINDOW,), lambda bi, i: (bi, 0, 0, prev_window(i))),
            pl.BlockSpec((1, halo_rows, CONV_CH), lambda bi, i: (bi, prev_rows(i), 1)),
            pl.BlockSpec((1, halo_rows, CONV_CH), lambda bi, i: (bi, prev_rows(i), 2)),
            _resident(conv_w.shape),
            _resident(bias.shape),
        ],
        out_specs=pl.BlockSpec((1, ts, n_out), lambda bi, i: (bi, i, 0)),
        out_shape=jax.ShapeDtypeStruct((b, s, n_out), jnp.bfloat16),
        compiler_params=_params("parallel", "parallel"),
        name="even_mixer",
    )(sinks, z, z, vt, vt, z, z, conv_w, bias)


def _moba_kernel(q_ref, k_ref, vt_ref, bias_ref, o_ref,
                 kaug_ref, kmean_ref, gate_ref, neg_ref, qaug_ref, e_ref,
                 *, q_blocks, chunk_blocks):
    step = pl.program_id(2)
    blk = MOBA_BLOCK
    dh = MOBA_HEAD_DIM
    nblk = k_ref.shape[1] // blk
    group = q_ref.shape[2] // dh
    rows = q_blocks * group * blk
    f32, bf16 = jnp.float32, jnp.bfloat16
    c1 = dh ** -0.5 * LOG2E

    @pl.when(step == 0)
    def _():
        onehot_row = lax.broadcasted_iota(jnp.int32, (blk, dh), 1)
        means = []
        for j in range(nblk):
            kj = k_ref[0, j * blk:(j + 1) * blk, :]
            kaug_ref[j * blk:(j + 1) * blk, 0:dh] = kj
            kaug_ref[j * blk:(j + 1) * blk, dh:2 * dh] = (onehot_row == j).astype(bf16)
            means.append(jnp.mean(kj.astype(f32), axis=0, keepdims=True))
        rest = jnp.concatenate(means, axis=0)
        for part in range(3):
            term = rest.astype(bf16)
            kmean_ref[part * nblk:(part + 1) * nblk, :] = term
            rest = rest - term.astype(f32)
        neg_ref[...] = jnp.zeros(neg_ref.shape, f32)

    q = q_ref[0]
    qs = jnp.concatenate([q[sb * blk:(sb + 1) * blk, g * dh:(g + 1) * dh]
                          for sb in range(q_blocks) for g in range(group)], axis=0)
    qaug_ref[:, 0:dh] = qs
    lane = lax.broadcasted_iota(jnp.int32, (1, rows), 1)
    own = step * q_blocks + lane // (group * blk)

    gate3 = _dot_nt(kmean_ref[...], qs)
    gate = gate3[0:nblk] + gate3[nblk:2 * nblk] + gate3[2 * nblk:3 * nblk]
    gate_ref[...] = gate
    blkid = lax.broadcasted_iota(jnp.int32, gate.shape, 0)
    past = blkid < own
    neg_ref[0:nblk, :] = jnp.zeros((nblk, rows), f32)

    def select(j, carry):
        gj = gate_ref[pl.ds(j, 1), :]
        beats = ((gate > gj) | ((gate == gj) & (blkid < j))) & past
        n_beat = jnp.sum(beats.astype(f32), axis=0, keepdims=True)
        keep = (n_beat < MOBA_TOPK) | (j >= own)
        neg_ref[pl.ds(j, 1), :] = jnp.where(keep, 0.0, MASKED)
        return carry

    lax.fori_loop(0, (step + 1) * q_blocks - 1, select, 0)
    qaug_ref[:, dh:2 * dh] = neg_ref[...].T.astype(bf16)
    qaug = qaug_ref[...]

    def attend(first_q_block):
        n_keys_blocks = first_q_block + q_blocks

        def tile_of(j, qb):
            return 3 if j > qb else min(qb - j, 2)

        m = None
        for c0 in range(0, n_keys_blocks, chunk_blocks):
            cb = min(chunk_blocks, n_keys_blocks - c0)
            s = _dot_nt(kaug_ref[c0 * blk:(c0 + cb) * blk, :], qaug)
            for r in range(cb):
                j = c0 + r
                bias = jnp.concatenate([bias_ref[0, tile_of(j, first_q_block + sb)]
                                        for sb in range(q_blocks)], axis=1)
                e = s[r * blk:(r + 1) * blk] * c1 + bias
                e_ref[j * blk:(j + 1) * blk, :] = e
                e_max = jnp.max(e, axis=0, keepdims=True)
                m = e_max if m is None else jnp.maximum(m, e_max)
        acc = jnp.zeros((vt_ref.shape[2], rows), f32)
        for j in range(n_keys_blocks):
            p = jnp.exp2(e_ref[j * blk:(j + 1) * blk, :] - m).astype(bf16)
            acc = acc + _dot(vt_ref[0, 0, :, j * blk:(j + 1) * blk], p)
        out = (acc[0:dh] / acc[dh:dh + 1]).T.astype(o_ref.dtype)
        for sb in range(q_blocks):
            for g in range(group):
                r0 = (sb * group + g) * blk
                o_ref[0, sb * blk:(sb + 1) * blk, g * dh:(g + 1) * dh] = out[r0:r0 + blk]

    for case in range(nblk // q_blocks):
        @pl.when(step == case)
        def _(case=case):
            attend(case * q_blocks)


def _moba(qk, vt, table, *, q_blocks=2, chunk_blocks=4):
    b, s, _ = qk.shape
    blk, dh = MOBA_BLOCK, MOBA_HEAD_DIM
    assert s % blk == 0 and dh == V7X_LANES
    nblk = s // blk
    assert nblk % q_blocks == 0 and nblk <= V7X_LANES
    group = MOBA_HEADS // MOBA_KV_HEADS
    rows = group * blk
    qi = np.arange(blk)[None, :]
    kj = np.arange(blk)[:, None]

    def transposed_tile(dist, visible):
        tile = _bias_tile(table, dist, visible) * LOG2E
        tile = tile.reshape(MOBA_KV_HEADS, group, blk, blk)
        return jnp.moveaxis(tile, 1, 2).reshape(MOBA_KV_HEADS, blk, rows)

    everything = np.ones((blk, blk), bool)
    nothing = np.zeros((blk, blk), bool)
    bias = jnp.stack([transposed_tile(qi - kj, qi - kj >= 0),
                      transposed_tile(qi + blk - kj, everything),
                      transposed_tile(qi + 2 * blk - kj, everything),
                      transposed_tile(qi - kj, nothing)], axis=1)
    assert (_rel_bucket(np.arange(blk + 1, 2 * blk * nblk)) == REL_BUCKETS - 1).all()
    k_block0 = MOBA_HEADS
    return pl.pallas_call(
        functools.partial(_moba_kernel, q_blocks=q_blocks, chunk_blocks=chunk_blocks),
        grid=(b, MOBA_KV_HEADS, nblk // q_blocks),
        in_specs=[
            pl.BlockSpec((1, q_blocks * blk, group * dh), lambda bi, h, i: (bi, i, h)),
            pl.BlockSpec((1, s, dh), lambda bi, h, i: (bi, 0, k_block0 + h)),
            pl.BlockSpec((1, 1, vt.shape[2], s), lambda bi, h, i: (bi, h, 0, 0)),
            pl.BlockSpec((1, 4, blk, rows), lambda bi, h, i: (h, 0, 0, 0)),
        ],
        out_specs=pl.BlockSpec((1, q_blocks * blk, group * dh), lambda bi, h, i: (bi, i, h)),
        out_shape=jax.ShapeDtypeStruct((b, s, MOBA_HEADS * dh), jnp.bfloat16),
        scratch_shapes=[
            pltpu.VMEM((s, 2 * dh), jnp.bfloat16),
            pltpu.VMEM((3 * nblk, dh), jnp.bfloat16),
            pltpu.VMEM((nblk, q_blocks * rows), jnp.float32),
            pltpu.VMEM((V7X_LANES, q_blocks * rows), jnp.float32),
            pltpu.VMEM((q_blocks * rows, 2 * dh), jnp.bfloat16),
            pltpu.VMEM((s, q_blocks * rows), jnp.float32),
        ],
        compiler_params=_params("parallel", "parallel", "arbitrary"),
        name="moba",
    )(qk, qk, vt, bias)


def _cross_attention(x, g_ref, wq_ref, kv_ref, wo_ref, a_ref):
    dh = XA_HEAD_DIM
    n = _rms(x, g_ref[...]).astype(jnp.bfloat16)
    q = _dot(n, wq_ref[...]).astype(jnp.bfloat16)
    scale = dh ** -0.5
    v0 = XA_HEADS * dh
    for h in range(XA_HEADS):
        cols = slice(h * dh, (h + 1) * dh)
        s = _dot_nt(q[:, cols], kv_ref[:, cols]) * scale
        m = jnp.max(s, axis=-1, keepdims=True)
        p = jnp.exp(s - m)
        l = jnp.sum(p, axis=-1, keepdims=True)
        out = _dot(p.astype(jnp.bfloat16), kv_ref[:, v0 + h * dh:v0 + (h + 1) * dh]) / l
        a_ref[:, cols] = out.astype(a_ref.dtype)
    return x + _dot(a_ref[...], wo_ref[...])


def _post_mixer_kernel(x_ref, y_ref, wmix_ref, xg_ref, wq_ref, kv_ref, wo_ref, fg_ref, wg_ref,
                       wu_ref, wd_ref, ng_ref, o_ref, a_ref, *, ff_chunk, final_norm):
    x = x_ref[...] + _dot(y_ref[...], wmix_ref[...])
    x = _cross_attention(x, xg_ref, wq_ref, kv_ref, wo_ref, a_ref)
    x = _half_swiglu(x, fg_ref, wg_ref, wu_ref, wd_ref, ff_chunk)
    if final_norm:
        x = _rms(x, ng_ref[...])
    o_ref[...] = x


def _post_mixer(x, y, w_mix, xa_g, w_q, kv, layer, w_o, ffn_g, wg, wu, wd, norm_g, *, final_norm,
                seq, mem_len, tm=512, ff_chunk=256):
    t, d = x.shape
    d_ff = wg.shape[1]
    xa_w = XA_HEADS * XA_HEAD_DIM
    assert seq % tm == 0 and d_ff % ff_chunk == 0
    tiles_per_seq = seq // tm
    row = pl.BlockSpec((tm, d), lambda i: (i, 0))
    return pl.pallas_call(
        functools.partial(_post_mixer_kernel, ff_chunk=ff_chunk, final_norm=final_norm),
        grid=(t // tm,),
        in_specs=[
            row,
            pl.BlockSpec((tm, y.shape[1]), lambda i: (i, 0)),
            _resident(w_mix.shape),
            _resident((1, d)),
            _resident((d, xa_w)),
            pl.BlockSpec((mem_len, 2 * xa_w), lambda i: (i // tiles_per_seq, layer)),
            _resident((xa_w, d)),
            _resident((1, d)),
            _resident((d, d_ff)),
            _resident((d, d_ff)),
            _resident((d_ff, d)),
            _resident((1, d)),
        ],
        out_specs=row,
        out_shape=jax.ShapeDtypeStruct((t, d), jnp.float32),
        scratch_shapes=[pltpu.VMEM((tm, xa_w), jnp.bfloat16)],
        compiler_params=_params("parallel"),
        name="post_mixer",
    )(x, y, w_mix, xa_g, w_q, kv, w_o, ffn_g, wg, wu, wd, norm_g)


def kernel(x, mem, ffn1_norm, ffn1_w_gate, ffn1_w_up, ffn1_w_down, mix_norm, ev_w_in, ev_conv_w,
           ev_sinks, ev_w_out, od_w_in, od_w_out, rel_bias, xa_norm, xa_w_q, xa_w_kv, xa_w_o,
           mem_norm, ffn2_norm, ffn2_w_gate, ffn2_w_up, ffn2_w_down, final_norm):
    b, s, d = x.shape
    depth = ffn1_norm.shape[0]
    mem_len = mem.shape[1]
    bf16 = jnp.bfloat16
    row = lambda g: g.reshape(1, d)

    kv_w = jnp.concatenate([xa_w_kv[l] for l in range(depth)], axis=1).astype(bf16)
    kv = _norm_proj(mem.reshape(b * mem_len, d), row(mem_norm), kv_w, tm=min(512, b * mem_len))

    xt = x.reshape(b * s, d)
    for l in range(depth):
        xt = _ffn(xt, row(ffn1_norm[l]), ffn1_w_gate[l].astype(bf16), ffn1_w_up[l].astype(bf16),
                  ffn1_w_down[l].astype(bf16))
        i = l // 2
        if l % 2 == 0:
            v0 = 3 * CONV_CH + (SWA_HEADS + SWA_KV_HEADS) * SWA_HEAD_DIM
            w_in = ev_w_in[i].astype(bf16)
            z, vt = _mixer_proj(xt, row(mix_norm[l]), w_in[:, :v0], w_in[:, v0:].T,
                                head_dim=SWA_HEAD_DIM, sum_rows=SWA_SUM_ROWS, seq=s)
            y = _even_mixer(z.reshape(b, s, -1), vt, ev_conv_w[i], ev_sinks[i], rel_bias)
            w_mix = ev_w_out[i]
        else:
            v0 = (MOBA_HEADS + MOBA_KV_HEADS) * MOBA_HEAD_DIM
            w_in = od_w_in[i].astype(bf16)
            qk, vt = _mixer_proj(xt, row(mix_norm[l]), w_in[:, :v0], w_in[:, v0:].T,
                                 head_dim=MOBA_HEAD_DIM, sum_rows=MOBA_SUM_ROWS, seq=s)
            y = _moba(qk.reshape(b, s, -1), vt, rel_bias)
            w_mix = od_w_out[i]
        xt = _post_mixer(xt, y.reshape(b * s, -1), w_mix.astype(bf16), row(xa_norm[l]),
                         xa_w_q[l].astype(bf16), kv, l, xa_w_o[l].astype(bf16), row(ffn2_norm[l]),
                         ffn2_w_gate[l].astype(bf16), ffn2_w_up[l].astype(bf16),
                         ffn2_w_down[l].astype(bf16), row(final_norm),
                         final_norm=(l == depth - 1), seq=s, mem_len=mem_len)
    return xt.reshape(b, s, d)
```

```python
import functools
import math

import jax
import jax.numpy as jnp
import numpy as np
from jax import lax
from jax.experimental import pallas as pl
from jax.experimental.pallas import tpu as pltpu

EPS = 1e-6
CONV_WIDTH = 3
CONV_CH = 512
SWA_HEADS = 8
SWA_KV_HEADS = 2
SWA_HEAD_DIM = 64
WINDOW = 128
MOBA_HEADS = 8
MOBA_KV_HEADS = 4
MOBA_HEAD_DIM = 128
MOBA_BLOCK = 256
MOBA_TOPK = 3
MOBA_SUM_ROWS = 16
SWA_SUM_ROWS = 64
REL_BUCKETS = 32
REL_MAX_DIST = 128
XA_HEADS = 4
XA_HEAD_DIM = 128

V7X_VMEM_BYTES = 64 * 1024 * 1024
VMEM_LIMIT_BYTES = V7X_VMEM_BYTES * 3 // 4
V7X_LANES = 128
MASKED = -1e30
LOG2E = math.log2(math.e)

_NT = (((1,), (1,)), ((), ()))


def _params(*semantics):
    return pltpu.CompilerParams(dimension_semantics=semantics, vmem_limit_bytes=VMEM_LIMIT_BYTES)


def _resident(shape, layer=None):
    zeros = (0,) * len(shape)
    if layer is None:
        return pl.BlockSpec(shape, lambda *_: zeros, pipeline_mode=pl.Buffered(1))
    return pl.BlockSpec((None,) + tuple(shape), lambda *_: (layer,) + zeros,
                        pipeline_mode=pl.Buffered(1))


def _rms(x, g):
    return x * lax.rsqrt(jnp.mean(x * x, axis=-1, keepdims=True) + EPS) * g


def _dot(a, b):
    return jnp.dot(a, b, preferred_element_type=jnp.float32)


def _dot_nt(a, b):
    return lax.dot_general(a, b, _NT, preferred_element_type=jnp.float32)


def _half_swiglu(x, g_ref, wg_ref, wu_ref, wd_ref, ff_chunk):
    n = _rms(x, g_ref[...]).astype(jnp.bfloat16)
    d_ff = wg_ref.shape[1]
    acc = jnp.zeros(x.shape, jnp.float32)
    for c in range(d_ff // ff_chunk):
        cols = slice(c * ff_chunk, (c + 1) * ff_chunk)
        gate = _dot(n, wg_ref[:, cols])
        up = _dot(n, wu_ref[:, cols])
        h = (gate * jax.nn.sigmoid(gate) * up).astype(jnp.bfloat16)
        acc = acc + _dot(h, wd_ref[cols, :])
    return x + 0.5 * acc


def _ffn_kernel(x_ref, g_ref, wg_ref, wu_ref, wd_ref, o_ref, *, ff_chunk):
    o_ref[...] = _half_swiglu(x_ref[...], g_ref, wg_ref, wu_ref, wd_ref, ff_chunk)


def _ffn(x, g, wg, wu, wd, layer, *, tm=512, ff_chunk=256):
    t, d = x.shape
    d_ff = wg.shape[2]
    assert t % tm == 0 and d_ff % ff_chunk == 0
    row = pl.BlockSpec((tm, d), lambda i: (i, 0))
    return pl.pallas_call(
        functools.partial(_ffn_kernel, ff_chunk=ff_chunk),
        grid=(t // tm,),
        in_specs=[row, _resident((1, d), layer), _resident((d, d_ff), layer),
                  _resident((d, d_ff), layer), _resident((d_ff, d), layer)],
        out_specs=row,
        out_shape=jax.ShapeDtypeStruct((t, d), jnp.float32),
        compiler_params=_params("parallel"),
        name="ffn",
    )(x, g, wg, wu, wd)


def _norm_proj_kernel(x_ref, g_ref, w_ref, o_ref):
    n = _rms(x_ref[...], g_ref[...]).astype(jnp.bfloat16)
    o_ref[...] = _dot(n, w_ref[...]).astype(o_ref.dtype)


def _norm_proj(x, g, w, *, tm=512):
    t, d = x.shape
    n_out = w.shape[1]
    assert t % tm == 0
    return pl.pallas_call(
        _norm_proj_kernel,
        grid=(t // tm,),
        in_specs=[pl.BlockSpec((tm, d), lambda i: (i, 0)), _resident((1, d)), _resident((d, n_out))],
        out_specs=pl.BlockSpec((tm, n_out), lambda i: (i, 0)),
        out_shape=jax.ShapeDtypeStruct((t, n_out), jnp.bfloat16),
        compiler_params=_params("parallel"),
        name="norm_proj",
    )(x, g, w)


def _mixer_proj_kernel(x_ref, g_ref, w_ref, wvt_ref, z_ref, vt_ref, *, head_dim):
    n = _rms(x_ref[...], g_ref[...]).astype(jnp.bfloat16)
    z_ref[...] = _dot(n, w_ref[...]).astype(z_ref.dtype)
    vt = _dot_nt(wvt_ref[...], n).astype(vt_ref.dtype)
    sum_rows = vt_ref.shape[2] - head_dim
    for h in range(vt_ref.shape[1]):
        vt_ref[0, h, 0:head_dim] = vt[h * head_dim:(h + 1) * head_dim, :]
        vt_ref[0, h, head_dim:] = jnp.ones((sum_rows, vt.shape[1]), vt_ref.dtype)


def _mixer_proj(x, g, w, w_vt, *, head_dim, sum_rows, seq, tm=512):
    t, d = x.shape
    n_out = w.shape[1]
    kv_heads = w_vt.shape[0] // head_dim
    rows = head_dim + sum_rows
    assert seq % tm == 0
    tiles_per_seq = seq // tm
    return pl.pallas_call(
        functools.partial(_mixer_proj_kernel, head_dim=head_dim),
        grid=(t // tm,),
        in_specs=[pl.BlockSpec((tm, d), lambda i: (i, 0)), _resident((1, d)), _resident(w.shape),
                  _resident(w_vt.shape)],
        out_specs=[
            pl.BlockSpec((tm, n_out), lambda i: (i, 0)),
            pl.BlockSpec((1, kv_heads, rows, tm),
                         lambda i: (i // tiles_per_seq, 0, 0, i % tiles_per_seq)),
        ],
        out_shape=[
            jax.ShapeDtypeStruct((t, n_out), jnp.bfloat16),
            jax.ShapeDtypeStruct((t // seq, kv_heads, rows, seq), jnp.bfloat16),
        ],
        compiler_params=_params("parallel"),
        name="mixer_proj",
    )(x, g, w, w_vt)


def _rel_bucket(dist):
    n = np.maximum(dist, 0)
    max_exact = REL_BUCKETS // 2
    nf = np.maximum(n, 1).astype(np.float32)
    scaled = (np.log(nf / np.float32(max_exact)) / np.float32(math.log(REL_MAX_DIST / max_exact))
              * np.float32(REL_BUCKETS - max_exact))
    large = np.minimum(max_exact + scaled.astype(np.int32), REL_BUCKETS - 1)
    return np.where(n < max_exact, n, large).astype(np.int32)


def _bias_tile(table, n_keys, n_queries, offset, visible):
    period = n_keys + n_queries
    j = np.arange(period)
    dist = np.where(j < n_queries, j, j - period) + offset
    vec = jnp.where(visible(dist)[:, None], table[_rel_bucket(dist)] * LOG2E, MASKED).T
    spread = jnp.tile(vec, (1, n_keys))[:, :n_keys * (period - 1)]
    return spread.reshape(-1, n_keys, period - 1)[:, :, :n_queries]


def _even_mixer_kernel(sinks_ref, z_ref, kh_ref, vt_ref, vth_ref, ch_ref, uh_ref, convw_ref,
                       bias_ref, y_ref, *, halo_rows):
    i = pl.program_id(1)
    ts = z_ref.shape[1]
    c1, c2, c3 = CONV_CH, 2 * CONV_CH, 3 * CONV_CH
    q0 = c3
    k0 = q0 + SWA_HEADS * SWA_HEAD_DIM
    group = SWA_HEADS // SWA_KV_HEADS
    dh = SWA_HEAD_DIM
    first = i == 0

    f32 = jnp.float32
    v = z_ref[0, :, c1:c2].astype(f32) * z_ref[0, :, c2:c3].astype(f32)
    vh = ch_ref[0].astype(f32) * uh_ref[0].astype(f32)
    vh = jnp.where(first, 0.0, vh)
    row = lax.broadcasted_iota(jnp.int32, v.shape, 0)
    h1 = vh[halo_rows - 1:halo_rows, :]
    h2 = vh[halo_rows - 2:halo_rows - 1, :]
    v1 = jnp.where(row == 0, h1, pltpu.roll(v, 1, 0))
    v2 = jnp.where(row == 0, h2, jnp.where(row == 1, h1, pltpu.roll(v, 2, 0)))
    conv = v2 * convw_ref[0:1, :] + v1 * convw_ref[1:2, :] + v * convw_ref[2:3, :]
    y_ref[0, :, 0:CONV_CH] = (z_ref[0, :, 0:c1].astype(f32) * conv).astype(y_ref.dtype)

    c_s = dh ** -0.5 * LOG2E
    lanes = group * WINDOW
    lane = lax.broadcasted_iota(jnp.int32, (1, lanes), 1)
    for kh in range(SWA_KV_HEADS):
        sink = jnp.zeros((1, lanes), f32)
        for g in range(group):
            sink = jnp.where(lane // WINDOW == g, sinks_ref[kh * group + g] * LOG2E, sink)
        kcol = slice(k0 + kh * dh, k0 + (kh + 1) * dh)
        for qb in range(ts // WINDOW):
            rows = slice(qb * WINDOW, (qb + 1) * WINDOW)
            if qb == 0:
                kc = jnp.concatenate([kh_ref[0, :, kh * dh:(kh + 1) * dh], z_ref[0, rows, kcol]],
                                     axis=0)
                vt = jnp.concatenate([vth_ref[0, kh], vt_ref[0, kh, :, 0:WINDOW]], axis=1)
                bias = bias_ref[kh, jnp.where(first, 1, 0)]
            else:
                kc = z_ref[0, (qb - 1) * WINDOW:(qb + 1) * WINDOW, kcol]
                vt = vt_ref[0, kh, :, (qb - 1) * WINDOW:(qb + 1) * WINDOW]
                bias = bias_ref[kh, 0]
            qs = jnp.concatenate(
                [z_ref[0, rows, q0 + (kh * group + g) * dh:q0 + (kh * group + g + 1) * dh]
                 for g in range(group)], axis=0)
            e = _dot_nt(kc, qs) * c_s + bias
            m = jnp.maximum(jnp.max(e, axis=0, keepdims=True), sink)
            p = jnp.exp2(e - m).astype(jnp.bfloat16)
            acc = _dot(vt, p)
            denom = acc[dh:dh + 1] + jnp.exp2(sink - m)
            out = (acc / denom).T
            for g in range(group):
                head = kh * group + g
                y_ref[0, rows, CONV_CH + head * dh:CONV_CH + (head + 1) * dh] = (
                    out[g * WINDOW:(g + 1) * WINDOW, 0:dh].astype(y_ref.dtype))


def _even_mixer(z, vt, conv_w, sinks, table, *, ts=512, halo_rows=16):
    b, s, n_in = z.shape
    n_out = CONV_CH + SWA_HEADS * SWA_HEAD_DIM
    group = SWA_HEADS // SWA_KV_HEADS
    assert s % ts == 0 and ts % WINDOW == 0 and vt.shape[2] == V7X_LANES
    banded = _bias_tile(table, 2 * WINDOW, WINDOW, WINDOW, lambda d: (d >= 0) & (d < WINDOW))
    own_only = jnp.where(np.arange(2 * WINDOW)[:, None] >= WINDOW, banded, MASKED)

    def by_kv_head(tile):
        tile = tile.reshape(SWA_KV_HEADS, group, 2 * WINDOW, WINDOW)
        return jnp.moveaxis(tile, 1, 2).reshape(SWA_KV_HEADS, 2 * WINDOW, group * WINDOW)

    bias = jnp.stack([by_kv_head(banded), by_kv_head(own_only)], axis=1)
    k_w = SWA_KV_HEADS * SWA_HEAD_DIM
    k_block = (3 * CONV_CH + SWA_HEADS * SWA_HEAD_DIM) // k_w
    w_per_tile = ts // WINDOW
    h_per_tile = ts // halo_rows
    prev_window = lambda i: jnp.maximum(i * w_per_tile - 1, 0)
    prev_rows = lambda i: jnp.maximum(i * h_per_tile - 1, 0)
    return pl.pallas_call(
        functools.partial(_even_mixer_kernel, halo_rows=halo_rows),
        grid=(b, s // ts),
        in_specs=[
            pl.BlockSpec(memory_space=pltpu.SMEM),
            pl.BlockSpec((1, ts, n_in), lambda bi, i: (bi, i, 0)),
            pl.BlockSpec((1, WINDOW, k_w), lambda bi, i: (bi, prev_window(i), k_block)),
            pl.BlockSpec((1,) + vt.shape[1:3] + (ts,), lambda bi, i: (bi, 0, 0, i)),
            pl.BlockSpec((1,) + vt.shape[1:3] + (WINDOW,), lambda bi, i: (bi, 0, 0, prev_window(i))),
            pl.BlockSpec((1, halo_rows, CONV_CH), lambda bi, i: (bi, prev_rows(i), 1)),
            pl.BlockSpec((1, halo_rows, CONV_CH), lambda bi, i: (bi, prev_rows(i), 2)),
            _resident(conv_w.shape),
            _resident(bias.shape),
        ],
        out_specs=pl.BlockSpec((1, ts, n_out), lambda bi, i: (bi, i, 0)),
        out_shape=jax.ShapeDtypeStruct((b, s, n_out), jnp.bfloat16),
        compiler_params=_params("parallel", "parallel"),
        name="even_mixer",
    )(sinks, z, z, vt, vt, z, z, conv_w, bias)


def _moba_kernel(q_ref, k_ref, vt_ref, bias_ref, o_ref,
                 kaug_ref, kmean_ref, gate_ref, neg_ref, qaug_ref, e_ref,
                 *, q_blocks, chunk_blocks):
    step = pl.program_id(2)
    blk = MOBA_BLOCK
    dh = MOBA_HEAD_DIM
    nblk = k_ref.shape[1] // blk
    group = q_ref.shape[2] // dh
    rows = q_blocks * group * blk
    f32, bf16 = jnp.float32, jnp.bfloat16
    c1 = dh ** -0.5 * LOG2E

    @pl.when(step == 0)
    def _():
        onehot_row = lax.broadcasted_iota(jnp.int32, (blk, dh), 1)
        means = []
        for j in range(nblk):
            kj = k_ref[0, j * blk:(j + 1) * blk, :]
            kaug_ref[j * blk:(j + 1) * blk, 0:dh] = kj
            kaug_ref[j * blk:(j + 1) * blk, dh:2 * dh] = (onehot_row == j).astype(bf16)
            means.append(jnp.mean(kj.astype(f32), axis=0, keepdims=True))
        rest = jnp.concatenate(means, axis=0)
        for part in range(3):
            term = rest.astype(bf16)
            kmean_ref[part * nblk:(part + 1) * nblk, :] = term
            rest = rest - term.astype(f32)
        neg_ref[...] = jnp.zeros(neg_ref.shape, f32)

    q = q_ref[0]
    qs = jnp.concatenate([q[sb * blk:(sb + 1) * blk, g * dh:(g + 1) * dh]
                          for sb in range(q_blocks) for g in range(group)], axis=0)
    qaug_ref[:, 0:dh] = qs
    lane = lax.broadcasted_iota(jnp.int32, (1, rows), 1)
    own = step * q_blocks + lane // (group * blk)

    gate3 = _dot_nt(kmean_ref[...], qs)
    gate = gate3[0:nblk] + gate3[nblk:2 * nblk] + gate3[2 * nblk:3 * nblk]
    gate_ref[...] = gate
    blkid = lax.broadcasted_iota(jnp.int32, gate.shape, 0)
    past = blkid < own
    neg_ref[0:nblk, :] = jnp.zeros((nblk, rows), f32)

    def select(j, carry):
        gj = gate_ref[pl.ds(j, 1), :]
        beats = ((gate > gj) | ((gate == gj) & (blkid < j))) & past
        n_beat = jnp.sum(beats.astype(f32), axis=0, keepdims=True)
        keep = (n_beat < MOBA_TOPK) | (j >= own)
        neg_ref[pl.ds(j, 1), :] = jnp.where(keep, 0.0, MASKED)
        return carry

    lax.fori_loop(0, (step + 1) * q_blocks - 1, select, 0)
    qaug_ref[:, dh:2 * dh] = neg_ref[...].T.astype(bf16)
    qaug = qaug_ref[...]

    def attend(first_q_block):
        n_keys_blocks = first_q_block + q_blocks

        def tile_of(j, qb):
            return 3 if j > qb else min(qb - j, 2)

        m = None
        for c0 in range(0, n_keys_blocks, chunk_blocks):
            cb = min(chunk_blocks, n_keys_blocks - c0)
            s = _dot_nt(kaug_ref[c0 * blk:(c0 + cb) * blk, :], qaug)
            for r in range(cb):
                j = c0 + r
                bias = jnp.concatenate([bias_ref[0, tile_of(j, first_q_block + sb)]
                                        for sb in range(q_blocks)], axis=1)
                e = s[r * blk:(r + 1) * blk] * c1 + bias
                e_ref[j * blk:(j + 1) * blk, :] = e
                e_max = jnp.max(e, axis=0, keepdims=True)
                m = e_max if m is None else jnp.maximum(m, e_max)
        acc = jnp.zeros((vt_ref.shape[2], rows), f32)
        for j in range(n_keys_blocks):
            p = jnp.exp2(e_ref[j * blk:(j + 1) * blk, :] - m).astype(bf16)
            acc = acc + _dot(vt_ref[0, 0, :, j * blk:(j + 1) * blk], p)
        out = (acc[0:dh] / acc[dh:dh + 1]).T.astype(o_ref.dtype)
        for sb in range(q_blocks):
            for g in range(group):
                r0 = (sb * group + g) * blk
                o_ref[0, sb * blk:(sb + 1) * blk, g * dh:(g + 1) * dh] = out[r0:r0 + blk]

    for case in range(nblk // q_blocks):
        @pl.when(step == case)
        def _(case=case):
            attend(case * q_blocks)


def _moba(qk, vt, table, *, q_blocks=2, chunk_blocks=4):
    b, s, _ = qk.shape
    blk, dh = MOBA_BLOCK, MOBA_HEAD_DIM
    assert s % blk == 0 and dh == V7X_LANES
    nblk = s // blk
    assert nblk % q_blocks == 0 and nblk <= V7X_LANES
    group = MOBA_HEADS // MOBA_KV_HEADS
    rows = group * blk

    def tile(blocks_back):
        t = _bias_tile(table, blk, blk, blocks_back * blk, lambda d: d >= 0)
        t = t.reshape(MOBA_KV_HEADS, group, blk, blk)
        return jnp.moveaxis(t, 1, 2).reshape(MOBA_KV_HEADS, blk, rows)

    bias = jnp.stack([tile(0), tile(1), tile(2), jnp.full((MOBA_KV_HEADS, blk, rows), MASKED)],
                     axis=1)
    assert (_rel_bucket(np.arange(blk + 1, 2 * blk * nblk)) == REL_BUCKETS - 1).all()
    k_block0 = MOBA_HEADS
    return pl.pallas_call(
        functools.partial(_moba_kernel, q_blocks=q_blocks, chunk_blocks=chunk_blocks),
        grid=(b, MOBA_KV_HEADS, nblk // q_blocks),
        in_specs=[
            pl.BlockSpec((1, q_blocks * blk, group * dh), lambda bi, h, i: (bi, i, h)),
            pl.BlockSpec((1, s, dh), lambda bi, h, i: (bi, 0, k_block0 + h)),
            pl.BlockSpec((1, 1, vt.shape[2], s), lambda bi, h, i: (bi, h, 0, 0)),
            pl.BlockSpec((1, 4, blk, rows), lambda bi, h, i: (h, 0, 0, 0)),
        ],
        out_specs=pl.BlockSpec((1, q_blocks * blk, group * dh), lambda bi, h, i: (bi, i, h)),
        out_shape=jax.ShapeDtypeStruct((b, s, MOBA_HEADS * dh), jnp.bfloat16),
        scratch_shapes=[
            pltpu.VMEM((s, 2 * dh), jnp.bfloat16),
            pltpu.VMEM((3 * nblk, dh), jnp.bfloat16),
            pltpu.VMEM((nblk, q_blocks * rows), jnp.float32),
            pltpu.VMEM((V7X_LANES, q_blocks * rows), jnp.float32),
            pltpu.VMEM((q_blocks * rows, 2 * dh), jnp.bfloat16),
            pltpu.VMEM((s, q_blocks * rows), jnp.float32),
        ],
        compiler_params=_params("parallel", "parallel", "arbitrary"),
        name="moba",
    )(qk, qk, vt, bias)


def _cross_attention(x, g_ref, wq_ref, kv_ref, wo_ref, a_ref):
    dh = XA_HEAD_DIM
    n = _rms(x, g_ref[...]).astype(jnp.bfloat16)
    q = _dot(n, wq_ref[...]).astype(jnp.bfloat16)
    scale = dh ** -0.5
    v0 = XA_HEADS * dh
    for h in range(XA_HEADS):
        cols = slice(h * dh, (h + 1) * dh)
        s = _dot_nt(q[:, cols], kv_ref[:, cols]) * scale
        m = jnp.max(s, axis=-1, keepdims=True)
        p = jnp.exp(s - m)
        l = jnp.sum(p, axis=-1, keepdims=True)
        out = _dot(p.astype(jnp.bfloat16), kv_ref[:, v0 + h * dh:v0 + (h + 1) * dh]) / l
        a_ref[:, cols] = out.astype(a_ref.dtype)
    return x + _dot(a_ref[...], wo_ref[...])


def _post_mixer_kernel(x_ref, y_ref, wmix_ref, xg_ref, wq_ref, kv_ref, wo_ref, fg_ref, wg_ref,
                       wu_ref, wd_ref, ng_ref, o_ref, a_ref, *, ff_chunk, final_norm):
    x = x_ref[...] + _dot(y_ref[...], wmix_ref[...])
    x = _cross_attention(x, xg_ref, wq_ref, kv_ref, wo_ref, a_ref)
    x = _half_swiglu(x, fg_ref, wg_ref, wu_ref, wd_ref, ff_chunk)
    if final_norm:
        x = _rms(x, ng_ref[...])
    o_ref[...] = x


def _post_mixer(x, y, w_mix, xa_g, w_q, kv, layer, w_o, ffn_g, wg, wu, wd, norm_g, *, final_norm,
                seq, mem_len, tm=512, ff_chunk=256):
    t, d = x.shape
    d_ff = wg.shape[2]
    xa_w = XA_HEADS * XA_HEAD_DIM
    assert seq % tm == 0 and d_ff % ff_chunk == 0
    tiles_per_seq = seq // tm
    row = pl.BlockSpec((tm, d), lambda i: (i, 0))
    return pl.pallas_call(
        functools.partial(_post_mixer_kernel, ff_chunk=ff_chunk, final_norm=final_norm),
        grid=(t // tm,),
        in_specs=[
            row,
            pl.BlockSpec((tm, y.shape[1]), lambda i: (i, 0)),
            _resident(w_mix.shape),
            _resident((1, d), layer),
            _resident((d, xa_w), layer),
            pl.BlockSpec((mem_len, 2 * xa_w), lambda i: (i // tiles_per_seq, layer)),
            _resident((xa_w, d), layer),
            _resident((1, d), layer),
            _resident((d, d_ff), layer),
            _resident((d, d_ff), layer),
            _resident((d_ff, d), layer),
            _resident((1, d)),
        ],
        out_specs=row,
        out_shape=jax.ShapeDtypeStruct((t, d), jnp.float32),
        scratch_shapes=[pltpu.VMEM((tm, xa_w), jnp.bfloat16)],
        compiler_params=_params("parallel"),
        name="post_mixer",
    )(x, y, w_mix, xa_g, w_q, kv, w_o, ffn_g, wg, wu, wd, norm_g)


def kernel(x, mem, ffn1_norm, ffn1_w_gate, ffn1_w_up, ffn1_w_down, mix_norm, ev_w_in, ev_conv_w,
           ev_sinks, ev_w_out, od_w_in, od_w_out, rel_bias, xa_norm, xa_w_q, xa_w_kv, xa_w_o,
           mem_norm, ffn2_norm, ffn2_w_gate, ffn2_w_up, ffn2_w_down, final_norm):
    b, s, d = x.shape
    depth = ffn1_norm.shape[0]
    mem_len = mem.shape[1]
    bf16 = jnp.bfloat16
    row = lambda g: g.reshape(1, d)

    kv_w = jnp.concatenate([xa_w_kv[l] for l in range(depth)], axis=1).astype(bf16)
    kv = _norm_proj(mem.reshape(b * mem_len, d), row(mem_norm), kv_w, tm=min(512, b * mem_len))

    gains = lambda g: g.reshape(depth, 1, d)
    ffn1 = (gains(ffn1_norm), ffn1_w_gate.astype(bf16), ffn1_w_up.astype(bf16), ffn1_w_down.astype(bf16))
    ffn2 = (gains(ffn2_norm), ffn2_w_gate.astype(bf16), ffn2_w_up.astype(bf16), ffn2_w_down.astype(bf16))
    xa_q, xa_o = xa_w_q.astype(bf16), xa_w_o.astype(bf16)

    xt = x.reshape(b * s, d)
    for l in range(depth):
        xt = _ffn(xt, *ffn1, l)
        i = l // 2
        if l % 2 == 0:
            v0 = 3 * CONV_CH + (SWA_HEADS + SWA_KV_HEADS) * SWA_HEAD_DIM
            w_in = ev_w_in[i].astype(bf16)
            z, vt = _mixer_proj(xt, row(mix_norm[l]), w_in[:, :v0], w_in[:, v0:].T,
                                head_dim=SWA_HEAD_DIM, sum_rows=SWA_SUM_ROWS, seq=s)
            y = _even_mixer(z.reshape(b, s, -1), vt, ev_conv_w[i], ev_sinks[i], rel_bias)
            w_mix = ev_w_out[i]
        else:
            v0 = (MOBA_HEADS + MOBA_KV_HEADS) * MOBA_HEAD_DIM
            w_in = od_w_in[i].astype(bf16)
            qk, vt = _mixer_proj(xt, row(mix_norm[l]), w_in[:, :v0], w_in[:, v0:].T,
                                 head_dim=MOBA_HEAD_DIM, sum_rows=MOBA_SUM_ROWS, seq=s)
            y = _moba(qk.reshape(b, s, -1), vt, rel_bias)
            w_mix = od_w_out[i]
        xt = _post_mixer(xt, y.reshape(b * s, -1), w_mix.astype(bf16), gains(xa_norm), xa_q, kv, l,
                         xa_o, *ffn2, row(final_norm), final_norm=(l == depth - 1), seq=s,
                         mem_len=mem_len)
    return xt.reshape(b, s, d)
```

```python
import functools
import math

import jax
import jax.numpy as jnp
import numpy as np
from jax import lax
from jax.experimental import pallas as pl
from jax.experimental.pallas import tpu as pltpu

EPS = 1e-6
CONV_WIDTH = 3
CONV_CH = 512
SWA_HEADS = 8
SWA_KV_HEADS = 2
SWA_HEAD_DIM = 64
WINDOW = 128
MOBA_HEADS = 8
MOBA_KV_HEADS = 4
MOBA_HEAD_DIM = 128
MOBA_BLOCK = 256
MOBA_TOPK = 3
MOBA_SUM_ROWS = 16
SWA_SUM_ROWS = 64
REL_BUCKETS = 32
REL_MAX_DIST = 128
XA_HEADS = 4
XA_HEAD_DIM = 128

V7X_VMEM_BYTES = 64 * 1024 * 1024
VMEM_LIMIT_BYTES = V7X_VMEM_BYTES * 3 // 4
V7X_LANES = 128
MASKED = -1e30
LOG2E = math.log2(math.e)

_NT = (((1,), (1,)), ((), ()))


def _params(*semantics):
    return pltpu.CompilerParams(dimension_semantics=semantics, vmem_limit_bytes=VMEM_LIMIT_BYTES)


def _resident(shape, layer=None):
    zeros = (0,) * len(shape)
    if layer is None:
        return pl.BlockSpec(shape, lambda *_: zeros, pipeline_mode=pl.Buffered(1))
    return pl.BlockSpec((None,) + tuple(shape), lambda *_: (layer,) + zeros,
                        pipeline_mode=pl.Buffered(1))


def _rms(x, g):
    return x * lax.rsqrt(jnp.mean(x * x, axis=-1, keepdims=True) + EPS) * g


def _dot(a, b):
    return jnp.dot(a, b, preferred_element_type=jnp.float32)


def _dot_nt(a, b):
    return lax.dot_general(a, b, _NT, preferred_element_type=jnp.float32)


def _half_swiglu(x, g_ref, wg_ref, wu_ref, wd_ref, ff_chunk):
    n = _rms(x, g_ref[...]).astype(jnp.bfloat16)
    d_ff = wg_ref.shape[1]
    acc = jnp.zeros(x.shape, jnp.float32)
    for c in range(d_ff // ff_chunk):
        cols = slice(c * ff_chunk, (c + 1) * ff_chunk)
        gate = _dot(n, wg_ref[:, cols])
        up = _dot(n, wu_ref[:, cols])
        h = (gate * jax.nn.sigmoid(gate) * up).astype(jnp.bfloat16)
        acc = acc + _dot(h, wd_ref[cols, :])
    return x + 0.5 * acc


def _ffn_kernel(x_ref, g_ref, wg_ref, wu_ref, wd_ref, o_ref, *, ff_chunk):
    o_ref[...] = _half_swiglu(x_ref[...], g_ref, wg_ref, wu_ref, wd_ref, ff_chunk)


def _ffn(x, g, wg, wu, wd, layer, *, tm=512, ff_chunk=256):
    t, d = x.shape
    d_ff = wg.shape[2]
    assert t % tm == 0 and d_ff % ff_chunk == 0
    row = pl.BlockSpec((tm, d), lambda i: (i, 0))
    return pl.pallas_call(
        functools.partial(_ffn_kernel, ff_chunk=ff_chunk),
        grid=(t // tm,),
        in_specs=[row, _resident((1, d), layer), _resident((d, d_ff), layer),
                  _resident((d, d_ff), layer), _resident((d_ff, d), layer)],
        out_specs=row,
        out_shape=jax.ShapeDtypeStruct((t, d), jnp.float32),
        compiler_params=_params("parallel"),
        name="ffn",
    )(x, g, wg, wu, wd)


def _norm_proj_kernel(x_ref, g_ref, w_ref, o_ref):
    n = _rms(x_ref[...], g_ref[...]).astype(jnp.bfloat16)
    o_ref[...] = _dot(n, w_ref[...]).astype(o_ref.dtype)


def _norm_proj(x, g, w, *, tm=512):
    t, d = x.shape
    n_out = w.shape[1]
    assert t % tm == 0
    return pl.pallas_call(
        _norm_proj_kernel,
        grid=(t // tm,),
        in_specs=[pl.BlockSpec((tm, d), lambda i: (i, 0)), _resident((1, d)), _resident((d, n_out))],
        out_specs=pl.BlockSpec((tm, n_out), lambda i: (i, 0)),
        out_shape=jax.ShapeDtypeStruct((t, n_out), jnp.bfloat16),
        compiler_params=_params("parallel"),
        name="norm_proj",
    )(x, g, w)


def _mixer_proj_kernel(x_ref, g_ref, w_ref, wvt_ref, z_ref, vt_ref, *, head_dim):
    n = _rms(x_ref[...], g_ref[...]).astype(jnp.bfloat16)
    z_ref[...] = _dot(n, w_ref[...]).astype(z_ref.dtype)
    vt = _dot_nt(wvt_ref[...], n).astype(vt_ref.dtype)
    sum_rows = vt_ref.shape[2] - head_dim
    for h in range(vt_ref.shape[1]):
        vt_ref[0, h, 0:head_dim] = vt[h * head_dim:(h + 1) * head_dim, :]
        vt_ref[0, h, head_dim:] = jnp.ones((sum_rows, vt.shape[1]), vt_ref.dtype)


def _mixer_proj(x, g, w, w_vt, *, head_dim, sum_rows, seq, tm=512):
    t, d = x.shape
    n_out = w.shape[1]
    kv_heads = w_vt.shape[0] // head_dim
    rows = head_dim + sum_rows
    assert seq % tm == 0
    tiles_per_seq = seq // tm
    return pl.pallas_call(
        functools.partial(_mixer_proj_kernel, head_dim=head_dim),
        grid=(t // tm,),
        in_specs=[pl.BlockSpec((tm, d), lambda i: (i, 0)), _resident((1, d)), _resident(w.shape),
                  _resident(w_vt.shape)],
        out_specs=[
            pl.BlockSpec((tm, n_out), lambda i: (i, 0)),
            pl.BlockSpec((1, kv_heads, rows, tm),
                         lambda i: (i // tiles_per_seq, 0, 0, i % tiles_per_seq)),
        ],
        out_shape=[
            jax.ShapeDtypeStruct((t, n_out), jnp.bfloat16),
            jax.ShapeDtypeStruct((t // seq, kv_heads, rows, seq), jnp.bfloat16),
        ],
        compiler_params=_params("parallel"),
        name="mixer_proj",
    )(x, g, w, w_vt)


def _rel_bucket(dist):
    n = np.maximum(dist, 0)
    max_exact = REL_BUCKETS // 2
    nf = np.maximum(n, 1).astype(np.float32)
    scaled = (np.log(nf / np.float32(max_exact)) / np.float32(math.log(REL_MAX_DIST / max_exact))
              * np.float32(REL_BUCKETS - max_exact))
    large = np.minimum(max_exact + scaled.astype(np.int32), REL_BUCKETS - 1)
    return np.where(n < max_exact, n, large).astype(np.int32)


def _bias_tile(table, n_keys, n_queries, offset, visible):
    period = n_keys + n_queries
    j = np.arange(period)
    dist = np.where(j < n_queries, j, j - period) + offset
    vec = jnp.where(visible(dist)[:, None], table[_rel_bucket(dist)] * LOG2E, MASKED).T
    spread = jnp.tile(vec, (1, n_keys))[:, :n_keys * (period - 1)]
    return spread.reshape(-1, n_keys, period - 1)[:, :, :n_queries]


def _even_mixer_kernel(sinks_ref, z_ref, kh_ref, vt_ref, vth_ref, ch_ref, uh_ref, convw_ref,
                       bias_ref, y_ref, *, halo_rows):
    i = pl.program_id(1)
    ts = z_ref.shape[1]
    c1, c2, c3 = CONV_CH, 2 * CONV_CH, 3 * CONV_CH
    q0 = c3
    k0 = q0 + SWA_HEADS * SWA_HEAD_DIM
    group = SWA_HEADS // SWA_KV_HEADS
    dh = SWA_HEAD_DIM
    first = i == 0

    f32 = jnp.float32
    v = z_ref[0, :, c1:c2].astype(f32) * z_ref[0, :, c2:c3].astype(f32)
    vh = ch_ref[0].astype(f32) * uh_ref[0].astype(f32)
    vh = jnp.where(first, 0.0, vh)
    row = lax.broadcasted_iota(jnp.int32, v.shape, 0)
    h1 = vh[halo_rows - 1:halo_rows, :]
    h2 = vh[halo_rows - 2:halo_rows - 1, :]
    v1 = jnp.where(row == 0, h1, pltpu.roll(v, 1, 0))
    v2 = jnp.where(row == 0, h2, jnp.where(row == 1, h1, pltpu.roll(v, 2, 0)))
    conv = v2 * convw_ref[0:1, :] + v1 * convw_ref[1:2, :] + v * convw_ref[2:3, :]
    y_ref[0, :, 0:CONV_CH] = (z_ref[0, :, 0:c1].astype(f32) * conv).astype(y_ref.dtype)

    c_s = dh ** -0.5 * LOG2E
    lanes = group * WINDOW
    lane = lax.broadcasted_iota(jnp.int32, (1, lanes), 1)
    for kh in range(SWA_KV_HEADS):
        sink = jnp.zeros((1, lanes), f32)
        for g in range(group):
            sink = jnp.where(lane // WINDOW == g, sinks_ref[kh * group + g] * LOG2E, sink)
        kcol = slice(k0 + kh * dh, k0 + (kh + 1) * dh)
        for qb in range(ts // WINDOW):
            rows = slice(qb * WINDOW, (qb + 1) * WINDOW)
            if qb == 0:
                kc = jnp.concatenate([kh_ref[0, :, kh * dh:(kh + 1) * dh], z_ref[0, rows, kcol]],
                                     axis=0)
                vt = jnp.concatenate([vth_ref[0, kh], vt_ref[0, kh, :, 0:WINDOW]], axis=1)
                bias = bias_ref[kh, jnp.where(first, 1, 0)]
            else:
                kc = z_ref[0, (qb - 1) * WINDOW:(qb + 1) * WINDOW, kcol]
                vt = vt_ref[0, kh, :, (qb - 1) * WINDOW:(qb + 1) * WINDOW]
                bias = bias_ref[kh, 0]
            qs = jnp.concatenate(
                [z_ref[0, rows, q0 + (kh * group + g) * dh:q0 + (kh * group + g + 1) * dh]
                 for g in range(group)], axis=0)
            e = _dot_nt(kc, qs) * c_s + bias
            m = jnp.maximum(jnp.max(e, axis=0, keepdims=True), sink)
            p = jnp.exp2(e - m).astype(jnp.bfloat16)
            acc = _dot(vt, p)
            denom = acc[dh:dh + 1] + jnp.exp2(sink - m)
            out = (acc / denom).T
            for g in range(group):
                head = kh * group + g
                y_ref[0, rows, CONV_CH + head * dh:CONV_CH + (head + 1) * dh] = (
                    out[g * WINDOW:(g + 1) * WINDOW, 0:dh].astype(y_ref.dtype))


def _even_mixer(z, vt, conv_w, sinks, table, *, ts=512, halo_rows=16):
    b, s, n_in = z.shape
    n_out = CONV_CH + SWA_HEADS * SWA_HEAD_DIM
    group = SWA_HEADS // SWA_KV_HEADS
    assert s % ts == 0 and ts % WINDOW == 0 and vt.shape[2] == V7X_LANES
    banded = _bias_tile(table, 2 * WINDOW, WINDOW, WINDOW, lambda d: (d >= 0) & (d < WINDOW))
    own_only = jnp.where(np.arange(2 * WINDOW)[:, None] >= WINDOW, banded, MASKED)

    def by_kv_head(tile):
        tile = tile.reshape(SWA_KV_HEADS, group, 2 * WINDOW, WINDOW)
        return jnp.moveaxis(tile, 1, 2).reshape(SWA_KV_HEADS, 2 * WINDOW, group * WINDOW)

    bias = jnp.stack([by_kv_head(banded), by_kv_head(own_only)], axis=1)
    k_w = SWA_KV_HEADS * SWA_HEAD_DIM
    k_block = (3 * CONV_CH + SWA_HEADS * SWA_HEAD_DIM) // k_w
    w_per_tile = ts // WINDOW
    h_per_tile = ts // halo_rows
    prev_window = lambda i: jnp.maximum(i * w_per_tile - 1, 0)
    prev_rows = lambda i: jnp.maximum(i * h_per_tile - 1, 0)
    return pl.pallas_call(
        functools.partial(_even_mixer_kernel, halo_rows=halo_rows),
        grid=(b, s // ts),
        in_specs=[
            pl.BlockSpec(memory_space=pltpu.SMEM),
            pl.BlockSpec((1, ts, n_in), lambda bi, i: (bi, i, 0)),
            pl.BlockSpec((1, WINDOW, k_w), lambda bi, i: (bi, prev_window(i), k_block)),
            pl.BlockSpec((1,) + vt.shape[1:3] + (ts,), lambda bi, i: (bi, 0, 0, i)),
            pl.BlockSpec((1,) + vt.shape[1:3] + (WINDOW,), lambda bi, i: (bi, 0, 0, prev_window(i))),
            pl.BlockSpec((1, halo_rows, CONV_CH), lambda bi, i: (bi, prev_rows(i), 1)),
            pl.BlockSpec((1, halo_rows, CONV_CH), lambda bi, i: (bi, prev_rows(i), 2)),
            _resident(conv_w.shape),
            _resident(bias.shape),
        ],
        out_specs=pl.BlockSpec((1, ts, n_out), lambda bi, i: (bi, i, 0)),
        out_shape=jax.ShapeDtypeStruct((b, s, n_out), jnp.bfloat16),
        compiler_params=_params("parallel", "parallel"),
        name="even_mixer",
    )(sinks, z, z, vt, vt, z, z, conv_w, bias)


def _moba_kernel(q_ref, k_ref, vt_ref, bias_ref, o_ref,
                 kaug_ref, kmean_ref, gate_ref, neg_ref, qaug_ref, e_ref,
                 *, q_blocks, chunk_blocks):
    step = pl.program_id(2)
    blk = MOBA_BLOCK
    dh = MOBA_HEAD_DIM
    nblk = k_ref.shape[1] // blk
    group = q_ref.shape[2] // dh
    rows = q_blocks * group * blk
    f32, bf16 = jnp.float32, jnp.bfloat16
    c1 = dh ** -0.5 * LOG2E

    @pl.when(step == 0)
    def _():
        onehot_row = lax.broadcasted_iota(jnp.int32, (blk, dh), 1)
        means = []
        for j in range(nblk):
            kj = k_ref[0, j * blk:(j + 1) * blk, :]
            kaug_ref[j * blk:(j + 1) * blk, 0:dh] = kj
            kaug_ref[j * blk:(j + 1) * blk, dh:2 * dh] = (onehot_row == j).astype(bf16)
            means.append(jnp.mean(kj.astype(f32), axis=0, keepdims=True))
        rest = jnp.concatenate(means, axis=0)
        for part in range(3):
            term = rest.astype(bf16)
            kmean_ref[part * nblk:(part + 1) * nblk, :] = term
            rest = rest - term.astype(f32)
        neg_ref[...] = jnp.zeros(neg_ref.shape, f32)

    q = q_ref[0]
    qs = jnp.concatenate([q[sb * blk:(sb + 1) * blk, g * dh:(g + 1) * dh]
                          for sb in range(q_blocks) for g in range(group)], axis=0)
    qaug_ref[:, 0:dh] = qs
    lane = lax.broadcasted_iota(jnp.int32, (1, rows), 1)
    own = step * q_blocks + lane // (group * blk)

    gate3 = _dot_nt(kmean_ref[...], qs)
    gate = gate3[0:nblk] + gate3[nblk:2 * nblk] + gate3[2 * nblk:3 * nblk]
    gate_ref[...] = gate
    blkid = lax.broadcasted_iota(jnp.int32, gate.shape, 0)
    past = blkid < own
    neg_ref[0:nblk, :] = jnp.zeros((nblk, rows), f32)

    def select(j, carry):
        gj = gate_ref[pl.ds(j, 1), :]
        beats = ((gate > gj) | ((gate == gj) & (blkid < j))) & past
        n_beat = jnp.sum(beats.astype(f32), axis=0, keepdims=True)
        keep = (n_beat < MOBA_TOPK) | (j >= own)
        neg_ref[pl.ds(j, 1), :] = jnp.where(keep, 0.0, MASKED)
        return carry

    lax.fori_loop(0, (step + 1) * q_blocks - 1, select, 0)
    qaug_ref[:, dh:2 * dh] = neg_ref[...].T.astype(bf16)

    def attend(first_q_block):
        width = group * blk

        def chunks(sb):
            n = first_q_block + sb + 1
            return [(c0, min(chunk_blocks, n - c0)) for c0 in range(0, n, chunk_blocks)]

        def exponents(sb, c0, cb, m):
            lanes = slice(sb * width, (sb + 1) * width)
            s = _dot_nt(kaug_ref[c0 * blk:(c0 + cb) * blk, :], qaug_ref[lanes, :])
            for r in range(cb):
                j = c0 + r
                tile = min(first_q_block + sb - j, 2)
                e = s[r * blk:(r + 1) * blk] * c1 + bias_ref[0, tile]
                e_ref[j * blk:(j + 1) * blk, lanes] = e
                e_max = jnp.max(e, axis=0, keepdims=True)
                m = e_max if m is None else jnp.maximum(m, e_max)
            return m

        def weighted_values(sb, j, m, acc):
            lanes = slice(sb * width, (sb + 1) * width)
            p = jnp.exp2(e_ref[j * blk:(j + 1) * blk, lanes] - m).astype(bf16)
            return acc + _dot(vt_ref[0, 0, :, j * blk:(j + 1) * blk], p)

        m = [None] * q_blocks
        acc = [jnp.zeros((vt_ref.shape[2], width), f32)] * q_blocks
        for sb in range(q_blocks + 1):
            pass1 = chunks(sb) if sb < q_blocks else []
            pass2 = chunks(sb - 1) if sb > 0 else []
            for ci in range(max(len(pass1), len(pass2))):
                if ci < len(pass1):
                    m[sb] = exponents(sb, *pass1[ci], m[sb])
                if ci < len(pass2):
                    c0, cb = pass2[ci]
                    for j in range(c0, c0 + cb):
                        acc[sb - 1] = weighted_values(sb - 1, j, m[sb - 1], acc[sb - 1])
        for sb in range(q_blocks):
            out = (acc[sb][0:dh] / acc[sb][dh:dh + 1]).T.astype(o_ref.dtype)
            for g in range(group):
                o_ref[0, sb * blk:(sb + 1) * blk, g * dh:(g + 1) * dh] = out[g * blk:(g + 1) * blk]

    for case in range(nblk // q_blocks):
        @pl.when(step == case)
        def _(case=case):
            attend(case * q_blocks)


def _moba(qk, vt, table, *, q_blocks=2, chunk_blocks=4):
    b, s, _ = qk.shape
    blk, dh = MOBA_BLOCK, MOBA_HEAD_DIM
    assert s % blk == 0 and dh == V7X_LANES
    nblk = s // blk
    assert nblk % q_blocks == 0 and nblk <= V7X_LANES
    group = MOBA_HEADS // MOBA_KV_HEADS
    rows = group * blk

    def tile(blocks_back):
        t = _bias_tile(table, blk, blk, blocks_back * blk, lambda d: d >= 0)
        t = t.reshape(MOBA_KV_HEADS, group, blk, blk)
        return jnp.moveaxis(t, 1, 2).reshape(MOBA_KV_HEADS, blk, rows)

    bias = jnp.stack([tile(0), tile(1), tile(2)], axis=1)
    assert (_rel_bucket(np.arange(blk + 1, 2 * blk * nblk)) == REL_BUCKETS - 1).all()
    k_block0 = MOBA_HEADS
    return pl.pallas_call(
        functools.partial(_moba_kernel, q_blocks=q_blocks, chunk_blocks=chunk_blocks),
        grid=(b, MOBA_KV_HEADS, nblk // q_blocks),
        in_specs=[
            pl.BlockSpec((1, q_blocks * blk, group * dh), lambda bi, h, i: (bi, i, h)),
            pl.BlockSpec((1, s, dh), lambda bi, h, i: (bi, 0, k_block0 + h)),
            pl.BlockSpec((1, 1, vt.shape[2], s), lambda bi, h, i: (bi, h, 0, 0)),
            pl.BlockSpec((1, 3, blk, rows), lambda bi, h, i: (h, 0, 0, 0)),
        ],
        out_specs=pl.BlockSpec((1, q_blocks * blk, group * dh), lambda bi, h, i: (bi, i, h)),
        out_shape=jax.ShapeDtypeStruct((b, s, MOBA_HEADS * dh), jnp.bfloat16),
        scratch_shapes=[
            pltpu.VMEM((s, 2 * dh), jnp.bfloat16),
            pltpu.VMEM((3 * nblk, dh), jnp.bfloat16),
            pltpu.VMEM((nblk, q_blocks * rows), jnp.float32),
            pltpu.VMEM((V7X_LANES, q_blocks * rows), jnp.float32),
            pltpu.VMEM((q_blocks * rows, 2 * dh), jnp.bfloat16),
            pltpu.VMEM((s, q_blocks * rows), jnp.float32),
        ],
        compiler_params=_params("parallel", "parallel", "arbitrary"),
        name="moba",
    )(qk, qk, vt, bias)


def _cross_attention(x, g_ref, wq_ref, kv_ref, wo_ref, a_ref):
    dh = XA_HEAD_DIM
    n = _rms(x, g_ref[...]).astype(jnp.bfloat16)
    q = _dot(n, wq_ref[...]).astype(jnp.bfloat16)
    scale = dh ** -0.5
    v0 = XA_HEADS * dh
    for h in range(XA_HEADS):
        cols = slice(h * dh, (h + 1) * dh)
        s = _dot_nt(q[:, cols], kv_ref[:, cols]) * scale
        m = jnp.max(s, axis=-1, keepdims=True)
        p = jnp.exp(s - m)
        l = jnp.sum(p, axis=-1, keepdims=True)
        out = _dot(p.astype(jnp.bfloat16), kv_ref[:, v0 + h * dh:v0 + (h + 1) * dh]) / l
        a_ref[:, cols] = out.astype(a_ref.dtype)
    return x + _dot(a_ref[...], wo_ref[...])


def _post_mixer_kernel(x_ref, y_ref, wmix_ref, xg_ref, wq_ref, kv_ref, wo_ref, fg_ref, wg_ref,
                       wu_ref, wd_ref, ng_ref, o_ref, a_ref, *, ff_chunk, final_norm):
    x = x_ref[...] + _dot(y_ref[...], wmix_ref[...])
    x = _cross_attention(x, xg_ref, wq_ref, kv_ref, wo_ref, a_ref)
    x = _half_swiglu(x, fg_ref, wg_ref, wu_ref, wd_ref, ff_chunk)
    if final_norm:
        x = _rms(x, ng_ref[...])
    o_ref[...] = x


def _post_mixer(x, y, w_mix, xa_g, w_q, kv, layer, w_o, ffn_g, wg, wu, wd, norm_g, *, final_norm,
                seq, mem_len, tm=512, ff_chunk=256):
    t, d = x.shape
    d_ff = wg.shape[2]
    xa_w = XA_HEADS * XA_HEAD_DIM
    assert seq % tm == 0 and d_ff % ff_chunk == 0
    tiles_per_seq = seq // tm
    row = pl.BlockSpec((tm, d), lambda i: (i, 0))
    return pl.pallas_call(
        functools.partial(_post_mixer_kernel, ff_chunk=ff_chunk, final_norm=final_norm),
        grid=(t // tm,),
        in_specs=[
            row,
            pl.BlockSpec((tm, y.shape[1]), lambda i: (i, 0)),
            _resident(w_mix.shape),
            _resident((1, d), layer),
            _resident((d, xa_w), layer),
            pl.BlockSpec((mem_len, 2 * xa_w), lambda i: (i // tiles_per_seq, layer)),
            _resident((xa_w, d), layer),
            _resident((1, d), layer),
            _resident((d, d_ff), layer),
            _resident((d, d_ff), layer),
            _resident((d_ff, d), layer),
            _resident((1, d)),
        ],
        out_specs=row,
        out_shape=jax.ShapeDtypeStruct((t, d), jnp.float32),
        scratch_shapes=[pltpu.VMEM((tm, xa_w), jnp.bfloat16)],
        compiler_params=_params("parallel"),
        name="post_mixer",
    )(x, y, w_mix, xa_g, w_q, kv, w_o, ffn_g, wg, wu, wd, norm_g)


def kernel(x, mem, ffn1_norm, ffn1_w_gate, ffn1_w_up, ffn1_w_down, mix_norm, ev_w_in, ev_conv_w,
           ev_sinks, ev_w_out, od_w_in, od_w_out, rel_bias, xa_norm, xa_w_q, xa_w_kv, xa_w_o,
           mem_norm, ffn2_norm, ffn2_w_gate, ffn2_w_up, ffn2_w_down, final_norm):
    b, s, d = x.shape
    depth = ffn1_norm.shape[0]
    mem_len = mem.shape[1]
    bf16 = jnp.bfloat16
    row = lambda g: g.reshape(1, d)

    kv_w = jnp.concatenate([xa_w_kv[l] for l in range(depth)], axis=1).astype(bf16)
    kv = _norm_proj(mem.reshape(b * mem_len, d), row(mem_norm), kv_w, tm=min(512, b * mem_len))

    gains = lambda g: g.reshape(depth, 1, d)
    ffn1 = (gains(ffn1_norm), ffn1_w_gate.astype(bf16), ffn1_w_up.astype(bf16), ffn1_w_down.astype(bf16))
    ffn2 = (gains(ffn2_norm), ffn2_w_gate.astype(bf16), ffn2_w_up.astype(bf16), ffn2_w_down.astype(bf16))
    xa_q, xa_o = xa_w_q.astype(bf16), xa_w_o.astype(bf16)

    xt = x.reshape(b * s, d)
    for l in range(depth):
        xt = _ffn(xt, *ffn1, l)
        i = l // 2
        if l % 2 == 0:
            v0 = 3 * CONV_CH + (SWA_HEADS + SWA_KV_HEADS) * SWA_HEAD_DIM
            w_in = ev_w_in[i].astype(bf16)
            z, vt = _mixer_proj(xt, row(mix_norm[l]), w_in[:, :v0], w_in[:, v0:].T,
                                head_dim=SWA_HEAD_DIM, sum_rows=SWA_SUM_ROWS, seq=s)
            y = _even_mixer(z.reshape(b, s, -1), vt, ev_conv_w[i], ev_sinks[i], rel_bias)
            w_mix = ev_w_out[i]
        else:
            v0 = (MOBA_HEADS + MOBA_KV_HEADS) * MOBA_HEAD_DIM
            w_in = od_w_in[i].astype(bf16)
            qk, vt = _mixer_proj(xt, row(mix_norm[l]), w_in[:, :v0], w_in[:, v0:].T,
                                 head_dim=MOBA_HEAD_DIM, sum_rows=MOBA_SUM_ROWS, seq=s)
            y = _moba(qk.reshape(b, s, -1), vt, rel_bias)
            w_mix = od_w_out[i]
        xt = _post_mixer(xt, y.reshape(b * s, -1), w_mix.astype(bf16), gains(xa_norm), xa_q, kv, l,
                         xa_o, *ffn2, row(final_norm), final_norm=(l == depth - 1), seq=s,
                         mem_len=mem_len)
    return xt.reshape(b, s, d)
```

```python
import functools
import math

import jax
import jax.numpy as jnp
import numpy as np
from jax import lax
from jax.experimental import pallas as pl
from jax.experimental.pallas import tpu as pltpu

EPS = 1e-6
CONV_WIDTH = 3
CONV_CH = 512
SWA_HEADS = 8
SWA_KV_HEADS = 2
SWA_HEAD_DIM = 64
WINDOW = 128
MOBA_HEADS = 8
MOBA_KV_HEADS = 4
MOBA_HEAD_DIM = 128
MOBA_BLOCK = 256
MOBA_TOPK = 3
MOBA_SUM_ROWS = 16
SWA_SUM_ROWS = 64
REL_BUCKETS = 32
REL_MAX_DIST = 128
XA_HEADS = 4
XA_HEAD_DIM = 128

V7X_VMEM_BYTES = 64 * 1024 * 1024
VMEM_LIMIT_BYTES = V7X_VMEM_BYTES * 3 // 4
V7X_LANES = 128
MASKED = -1e30
LOG2E = math.log2(math.e)

_NT = (((1,), (1,)), ((), ()))


def _params(*semantics):
    return pltpu.CompilerParams(dimension_semantics=semantics, vmem_limit_bytes=VMEM_LIMIT_BYTES)


def _resident(shape, layer=None):
    zeros = (0,) * len(shape)
    if layer is None:
        return pl.BlockSpec(shape, lambda *_: zeros, pipeline_mode=pl.Buffered(1))
    return pl.BlockSpec((None,) + tuple(shape), lambda *_: (layer,) + zeros,
                        pipeline_mode=pl.Buffered(1))


def _rms(x, g):
    return x * lax.rsqrt(jnp.mean(x * x, axis=-1, keepdims=True) + EPS) * g


def _dot(a, b):
    return jnp.dot(a, b, preferred_element_type=jnp.float32)


def _dot_nt(a, b):
    return lax.dot_general(a, b, _NT, preferred_element_type=jnp.float32)


def _half_swiglu(x, g_ref, wg_ref, wu_ref, wd_ref, ff_chunk):
    n = _rms(x, g_ref[...]).astype(jnp.bfloat16)
    d_ff = wg_ref.shape[1]
    acc = jnp.zeros(x.shape, jnp.float32)
    for c in range(d_ff // ff_chunk):
        cols = slice(c * ff_chunk, (c + 1) * ff_chunk)
        gate = _dot(n, wg_ref[:, cols])
        up = _dot(n, wu_ref[:, cols])
        h = (gate * jax.nn.sigmoid(gate) * up).astype(jnp.bfloat16)
        acc = acc + _dot(h, wd_ref[cols, :])
    return x + 0.5 * acc


def _ffn_kernel(x_ref, g_ref, wg_ref, wu_ref, wd_ref, o_ref, *, ff_chunk):
    o_ref[...] = _half_swiglu(x_ref[...], g_ref, wg_ref, wu_ref, wd_ref, ff_chunk)


def _ffn(x, g, wg, wu, wd, layer, *, tm=512, ff_chunk=256):
    t, d = x.shape
    d_ff = wg.shape[2]
    assert t % tm == 0 and d_ff % ff_chunk == 0
    row = pl.BlockSpec((tm, d), lambda i: (i, 0))
    return pl.pallas_call(
        functools.partial(_ffn_kernel, ff_chunk=ff_chunk),
        grid=(t // tm,),
        in_specs=[row, _resident((1, d), layer), _resident((d, d_ff), layer),
                  _resident((d, d_ff), layer), _resident((d_ff, d), layer)],
        out_specs=row,
        out_shape=jax.ShapeDtypeStruct((t, d), jnp.float32),
        compiler_params=_params("parallel"),
        name="ffn",
    )(x, g, wg, wu, wd)


def _norm_proj_kernel(x_ref, g_ref, w_ref, o_ref):
    n = _rms(x_ref[...], g_ref[...]).astype(jnp.bfloat16)
    o_ref[...] = _dot(n, w_ref[...]).astype(o_ref.dtype)


def _norm_proj(x, g, w, *, tm=512):
    t, d = x.shape
    n_out = w.shape[1]
    assert t % tm == 0
    return pl.pallas_call(
        _norm_proj_kernel,
        grid=(t // tm,),
        in_specs=[pl.BlockSpec((tm, d), lambda i: (i, 0)), _resident((1, d)), _resident((d, n_out))],
        out_specs=pl.BlockSpec((tm, n_out), lambda i: (i, 0)),
        out_shape=jax.ShapeDtypeStruct((t, n_out), jnp.bfloat16),
        compiler_params=_params("parallel"),
        name="norm_proj",
    )(x, g, w)


def _mixer_proj_kernel(x_ref, g_ref, w_ref, wvt_ref, z_ref, vt_ref, *, head_dim):
    n = _rms(x_ref[...], g_ref[...]).astype(jnp.bfloat16)
    z_ref[...] = _dot(n, w_ref[...]).astype(z_ref.dtype)
    vt = _dot_nt(wvt_ref[...], n).astype(vt_ref.dtype)
    sum_rows = vt_ref.shape[2] - head_dim
    for h in range(vt_ref.shape[1]):
        vt_ref[0, h, 0:head_dim] = vt[h * head_dim:(h + 1) * head_dim, :]
        vt_ref[0, h, head_dim:] = jnp.ones((sum_rows, vt.shape[1]), vt_ref.dtype)


def _mixer_proj(x, g, w, w_vt, *, head_dim, sum_rows, seq, tm=512):
    t, d = x.shape
    n_out = w.shape[1]
    kv_heads = w_vt.shape[0] // head_dim
    rows = head_dim + sum_rows
    assert seq % tm == 0
    tiles_per_seq = seq // tm
    return pl.pallas_call(
        functools.partial(_mixer_proj_kernel, head_dim=head_dim),
        grid=(t // tm,),
        in_specs=[pl.BlockSpec((tm, d), lambda i: (i, 0)), _resident((1, d)), _resident(w.shape),
                  _resident(w_vt.shape)],
        out_specs=[
            pl.BlockSpec((tm, n_out), lambda i: (i, 0)),
            pl.BlockSpec((1, kv_heads, rows, tm),
                         lambda i: (i // tiles_per_seq, 0, 0, i % tiles_per_seq)),
        ],
        out_shape=[
            jax.ShapeDtypeStruct((t, n_out), jnp.bfloat16),
            jax.ShapeDtypeStruct((t // seq, kv_heads, rows, seq), jnp.bfloat16),
        ],
        compiler_params=_params("parallel"),
        name="mixer_proj",
    )(x, g, w, w_vt)


def _rel_bucket(dist):
    n = np.maximum(dist, 0)
    max_exact = REL_BUCKETS // 2
    nf = np.maximum(n, 1).astype(np.float32)
    scaled = (np.log(nf / np.float32(max_exact)) / np.float32(math.log(REL_MAX_DIST / max_exact))
              * np.float32(REL_BUCKETS - max_exact))
    large = np.minimum(max_exact + scaled.astype(np.int32), REL_BUCKETS - 1)
    return np.where(n < max_exact, n, large).astype(np.int32)


def _bias_tile(table, n_keys, n_queries, offset, visible):
    period = n_keys + n_queries
    j = np.arange(period)
    dist = np.where(j < n_queries, j, j - period) + offset
    vec = jnp.where(visible(dist)[:, None], table[_rel_bucket(dist)] * LOG2E, MASKED).T
    spread = jnp.tile(vec, (1, n_keys))[:, :n_keys * (period - 1)]
    return spread.reshape(-1, n_keys, period - 1)[:, :, :n_queries]


def _even_mixer_kernel(sinks_ref, z_ref, kh_ref, vt_ref, vth_ref, ch_ref, uh_ref, convw_ref,
                       bias_ref, y_ref, *, halo_rows):
    i = pl.program_id(1)
    ts = z_ref.shape[1]
    c1, c2, c3 = CONV_CH, 2 * CONV_CH, 3 * CONV_CH
    q0 = c3
    k0 = q0 + SWA_HEADS * SWA_HEAD_DIM
    group = SWA_HEADS // SWA_KV_HEADS
    dh = SWA_HEAD_DIM
    first = i == 0

    f32 = jnp.float32
    v = z_ref[0, :, c1:c2].astype(f32) * z_ref[0, :, c2:c3].astype(f32)
    vh = ch_ref[0].astype(f32) * uh_ref[0].astype(f32)
    vh = jnp.where(first, 0.0, vh)
    row = lax.broadcasted_iota(jnp.int32, v.shape, 0)
    h1 = vh[halo_rows - 1:halo_rows, :]
    h2 = vh[halo_rows - 2:halo_rows - 1, :]
    v1 = jnp.where(row == 0, h1, pltpu.roll(v, 1, 0))
    v2 = jnp.where(row == 0, h2, jnp.where(row == 1, h1, pltpu.roll(v, 2, 0)))
    conv = v2 * convw_ref[0:1, :] + v1 * convw_ref[1:2, :] + v * convw_ref[2:3, :]
    y_ref[0, :, 0:CONV_CH] = (z_ref[0, :, 0:c1].astype(f32) * conv).astype(y_ref.dtype)

    c_s = dh ** -0.5 * LOG2E
    lanes = group * WINDOW
    lane = lax.broadcasted_iota(jnp.int32, (1, lanes), 1)
    for kh in range(SWA_KV_HEADS):
        sink = jnp.zeros((1, lanes), f32)
        for g in range(group):
            sink = jnp.where(lane // WINDOW == g, sinks_ref[kh * group + g] * LOG2E, sink)
        kcol = slice(k0 + kh * dh, k0 + (kh + 1) * dh)
        for qb in range(ts // WINDOW):
            rows = slice(qb * WINDOW, (qb + 1) * WINDOW)
            if qb == 0:
                kc = jnp.concatenate([kh_ref[0, :, kh * dh:(kh + 1) * dh], z_ref[0, rows, kcol]],
                                     axis=0)
                vt = jnp.concatenate([vth_ref[0, kh], vt_ref[0, kh, :, 0:WINDOW]], axis=1)
                bias = bias_ref[kh, jnp.where(first, 1, 0)]
            else:
                kc = z_ref[0, (qb - 1) * WINDOW:(qb + 1) * WINDOW, kcol]
                vt = vt_ref[0, kh, :, (qb - 1) * WINDOW:(qb + 1) * WINDOW]
                bias = bias_ref[kh, 0]
            qs = jnp.concatenate(
                [z_ref[0, rows, q0 + (kh * group + g) * dh:q0 + (kh * group + g + 1) * dh]
                 for g in range(group)], axis=0)
            e = _dot_nt(kc, qs) * c_s + bias
            m = jnp.maximum(jnp.max(e, axis=0, keepdims=True), sink)
            p = jnp.exp2(e - m).astype(jnp.bfloat16)
            acc = _dot(vt, p)
            denom = acc[dh:dh + 1] + jnp.exp2(sink - m)
            out = (acc / denom).T
            for g in range(group):
                head = kh * group + g
                y_ref[0, rows, CONV_CH + head * dh:CONV_CH + (head + 1) * dh] = (
                    out[g * WINDOW:(g + 1) * WINDOW, 0:dh].astype(y_ref.dtype))


def _even_mixer(z, vt, conv_w, sinks, table, *, ts=512, halo_rows=16):
    b, s, n_in = z.shape
    n_out = CONV_CH + SWA_HEADS * SWA_HEAD_DIM
    group = SWA_HEADS // SWA_KV_HEADS
    assert s % ts == 0 and ts % WINDOW == 0 and vt.shape[2] == V7X_LANES
    banded = _bias_tile(table, 2 * WINDOW, WINDOW, WINDOW, lambda d: (d >= 0) & (d < WINDOW))
    own_only = jnp.where(np.arange(2 * WINDOW)[:, None] >= WINDOW, banded, MASKED)

    def by_kv_head(tile):
        tile = tile.reshape(SWA_KV_HEADS, group, 2 * WINDOW, WINDOW)
        return jnp.moveaxis(tile, 1, 2).reshape(SWA_KV_HEADS, 2 * WINDOW, group * WINDOW)

    bias = jnp.stack([by_kv_head(banded), by_kv_head(own_only)], axis=1)
    k_w = SWA_KV_HEADS * SWA_HEAD_DIM
    k_block = (3 * CONV_CH + SWA_HEADS * SWA_HEAD_DIM) // k_w
    w_per_tile = ts // WINDOW
    h_per_tile = ts // halo_rows
    prev_window = lambda i: jnp.maximum(i * w_per_tile - 1, 0)
    prev_rows = lambda i: jnp.maximum(i * h_per_tile - 1, 0)
    return pl.pallas_call(
        functools.partial(_even_mixer_kernel, halo_rows=halo_rows),
        grid=(b, s // ts),
        in_specs=[
            pl.BlockSpec(memory_space=pltpu.SMEM),
            pl.BlockSpec((1, ts, n_in), lambda bi, i: (bi, i, 0)),
            pl.BlockSpec((1, WINDOW, k_w), lambda bi, i: (bi, prev_window(i), k_block)),
            pl.BlockSpec((1,) + vt.shape[1:3] + (ts,), lambda bi, i: (bi, 0, 0, i)),
            pl.BlockSpec((1,) + vt.shape[1:3] + (WINDOW,), lambda bi, i: (bi, 0, 0, prev_window(i))),
            pl.BlockSpec((1, halo_rows, CONV_CH), lambda bi, i: (bi, prev_rows(i), 1)),
            pl.BlockSpec((1, halo_rows, CONV_CH), lambda bi, i: (bi, prev_rows(i), 2)),
            _resident(conv_w.shape),
            _resident(bias.shape),
        ],
        out_specs=pl.BlockSpec((1, ts, n_out), lambda bi, i: (bi, i, 0)),
        out_shape=jax.ShapeDtypeStruct((b, s, n_out), jnp.bfloat16),
        compiler_params=_params("parallel", "parallel"),
        name="even_mixer",
    )(sinks, z, z, vt, vt, z, z, conv_w, bias)


def _moba_kernel(q_ref, k_ref, vt_ref, bias_ref, o_ref,
                 kaug_ref, kmean_ref, neg_ref, qaug_ref, e_ref,
                 *, q_blocks, chunk_blocks):
    step = pl.program_id(2)
    blk = MOBA_BLOCK
    dh = MOBA_HEAD_DIM
    nblk = k_ref.shape[1] // blk
    group = q_ref.shape[2] // dh
    rows = q_blocks * group * blk
    f32, bf16 = jnp.float32, jnp.bfloat16
    c1 = dh ** -0.5 * LOG2E

    @pl.when(step == 0)
    def _():
        onehot_row = lax.broadcasted_iota(jnp.int32, (blk, dh), 1)
        means = []
        for j in range(nblk):
            kj = k_ref[0, j * blk:(j + 1) * blk, :]
            kaug_ref[j * blk:(j + 1) * blk, 0:dh] = kj
            kaug_ref[j * blk:(j + 1) * blk, dh:2 * dh] = (onehot_row == j).astype(bf16)
            means.append(jnp.mean(kj.astype(f32), axis=0, keepdims=True))
        rest = jnp.concatenate(means, axis=0)
        for part in range(3):
            term = rest.astype(bf16)
            kmean_ref[part * nblk:(part + 1) * nblk, :] = term
            rest = rest - term.astype(f32)
        neg_ref[...] = jnp.zeros(neg_ref.shape, f32)

    q = q_ref[0]
    qs = jnp.concatenate([q[sb * blk:(sb + 1) * blk, g * dh:(g + 1) * dh]
                          for sb in range(q_blocks) for g in range(group)], axis=0)
    qaug_ref[:, 0:dh] = qs
    lane = lax.broadcasted_iota(jnp.int32, (1, rows), 1)
    own = step * q_blocks + lane // (group * blk)

    gate3 = _dot_nt(kmean_ref[...], qs)
    gate = gate3[0:nblk] + gate3[nblk:2 * nblk] + gate3[2 * nblk:3 * nblk]
    blkid = lax.broadcasted_iota(jnp.int32, gate.shape, 0)
    remaining = jnp.where(blkid < own, gate, -jnp.inf)
    keep = blkid >= own
    for _ in range(MOBA_TOPK):
        best = jnp.max(remaining, axis=0, keepdims=True)
        first = jnp.min(jnp.where(remaining == best, blkid, nblk), axis=0, keepdims=True)
        taken = (blkid == first) & (best > -jnp.inf)
        keep = keep | taken
        remaining = jnp.where(taken, -jnp.inf, remaining)
    neg_ref[0:nblk, :] = jnp.where(keep, 0.0, MASKED)
    qaug_ref[:, dh:2 * dh] = neg_ref[...].T.astype(bf16)

    def attend(first_q_block):
        width = group * blk

        def chunks(sb):
            n = first_q_block + sb + 1
            return [(c0, min(chunk_blocks, n - c0)) for c0 in range(0, n, chunk_blocks)]

        def exponents(sb, c0, cb, m):
            lanes = slice(sb * width, (sb + 1) * width)
            s = _dot_nt(kaug_ref[c0 * blk:(c0 + cb) * blk, :], qaug_ref[lanes, :])
            for r in range(cb):
                j = c0 + r
                tile = min(first_q_block + sb - j, 2)
                e = s[r * blk:(r + 1) * blk] * c1 + bias_ref[0, tile]
                e_ref[sb % 2, j * blk:(j + 1) * blk, :] = e
                e_max = jnp.max(e, axis=0, keepdims=True)
                m = e_max if m is None else jnp.maximum(m, e_max)
            return m

        def weighted_values(sb, j, m, acc):
            p = jnp.exp2(e_ref[sb % 2, j * blk:(j + 1) * blk, :] - m).astype(bf16)
            return acc + _dot(vt_ref[0, 0, :, j * blk:(j + 1) * blk], p)

        m = [None] * q_blocks
        for sb in range(q_blocks + 1):
            pass1 = chunks(sb) if sb < q_blocks else []
            pass2 = chunks(sb - 1) if sb > 0 else []
            acc = jnp.zeros((vt_ref.shape[2], width), f32)
            for ci in range(max(len(pass1), len(pass2))):
                if ci < len(pass1):
                    m[sb] = exponents(sb, *pass1[ci], m[sb])
                if ci < len(pass2):
                    c0, cb = pass2[ci]
                    for j in range(c0, c0 + cb):
                        acc = weighted_values(sb - 1, j, m[sb - 1], acc)
            if sb > 0:
                out = (acc[0:dh] / acc[dh:dh + 1]).T.astype(o_ref.dtype)
                rows_out = slice((sb - 1) * blk, sb * blk)
                for g in range(group):
                    o_ref[0, rows_out, g * dh:(g + 1) * dh] = out[g * blk:(g + 1) * blk]

    for case in range(nblk // q_blocks):
        @pl.when(step == case)
        def _(case=case):
            attend(case * q_blocks)


def _moba(qk, vt, table, *, q_blocks=4, chunk_blocks=8):
    b, s, _ = qk.shape
    blk, dh = MOBA_BLOCK, MOBA_HEAD_DIM
    assert s % blk == 0 and dh == V7X_LANES
    nblk = s // blk
    q_blocks = q_blocks or nblk
    assert nblk % q_blocks == 0 and nblk <= V7X_LANES
    group = MOBA_HEADS // MOBA_KV_HEADS
    rows = group * blk

    def tile(blocks_back):
        t = _bias_tile(table, blk, blk, blocks_back * blk, lambda d: d >= 0)
        t = t.reshape(MOBA_KV_HEADS, group, blk, blk)
        return jnp.moveaxis(t, 1, 2).reshape(MOBA_KV_HEADS, blk, rows)

    bias = jnp.stack([tile(0), tile(1), tile(2)], axis=1)
    assert (_rel_bucket(np.arange(blk + 1, 2 * blk * nblk)) == REL_BUCKETS - 1).all()
    k_block0 = MOBA_HEADS
    return pl.pallas_call(
        functools.partial(_moba_kernel, q_blocks=q_blocks, chunk_blocks=chunk_blocks),
        grid=(b, MOBA_KV_HEADS, nblk // q_blocks),
        in_specs=[
            pl.BlockSpec((1, q_blocks * blk, group * dh), lambda bi, h, i: (bi, i, h)),
            pl.BlockSpec((1, s, dh), lambda bi, h, i: (bi, 0, k_block0 + h)),
            pl.BlockSpec((1, 1, vt.shape[2], s), lambda bi, h, i: (bi, h, 0, 0)),
            pl.BlockSpec((1, 3, blk, rows), lambda bi, h, i: (h, 0, 0, 0)),
        ],
        out_specs=pl.BlockSpec((1, q_blocks * blk, group * dh), lambda bi, h, i: (bi, i, h)),
        out_shape=jax.ShapeDtypeStruct((b, s, MOBA_HEADS * dh), jnp.bfloat16),
        scratch_shapes=[
            pltpu.VMEM((s, 2 * dh), jnp.bfloat16),
            pltpu.VMEM((3 * nblk, dh), jnp.bfloat16),
            pltpu.VMEM((V7X_LANES, q_blocks * rows), jnp.float32),
            pltpu.VMEM((q_blocks * rows, 2 * dh), jnp.bfloat16),
            pltpu.VMEM((2, s, rows), jnp.float32),
        ],
        compiler_params=_params("parallel", "parallel", "arbitrary"),
        name="moba",
    )(qk, qk, vt, bias)


def _cross_attention(x, g_ref, wq_ref, kv_ref, wo_ref, a_ref):
    dh = XA_HEAD_DIM
    n = _rms(x, g_ref[...]).astype(jnp.bfloat16)
    q = _dot(n, wq_ref[...]).astype(jnp.bfloat16)
    scale = dh ** -0.5
    v0 = XA_HEADS * dh
    for h in range(XA_HEADS):
        cols = slice(h * dh, (h + 1) * dh)
        s = _dot_nt(q[:, cols], kv_ref[:, cols]) * scale
        m = jnp.max(s, axis=-1, keepdims=True)
        p = jnp.exp(s - m)
        l = jnp.sum(p, axis=-1, keepdims=True)
        out = _dot(p.astype(jnp.bfloat16), kv_ref[:, v0 + h * dh:v0 + (h + 1) * dh]) / l
        a_ref[:, cols] = out.astype(a_ref.dtype)
    return x + _dot(a_ref[...], wo_ref[...])


def _post_mixer_kernel(x_ref, y_ref, wmix_ref, xg_ref, wq_ref, kv_ref, wo_ref, fg_ref, wg_ref,
                       wu_ref, wd_ref, ng_ref, o_ref, a_ref, *, ff_chunk, final_norm):
    x = x_ref[...] + _dot(y_ref[...], wmix_ref[...])
    x = _cross_attention(x, xg_ref, wq_ref, kv_ref, wo_ref, a_ref)
    x = _half_swiglu(x, fg_ref, wg_ref, wu_ref, wd_ref, ff_chunk)
    if final_norm:
        x = _rms(x, ng_ref[...])
    o_ref[...] = x


def _post_mixer(x, y, w_mix, xa_g, w_q, kv, layer, w_o, ffn_g, wg, wu, wd, norm_g, *, final_norm,
                seq, mem_len, tm=512, ff_chunk=256):
    t, d = x.shape
    d_ff = wg.shape[2]
    xa_w = XA_HEADS * XA_HEAD_DIM
    assert seq % tm == 0 and d_ff % ff_chunk == 0
    tiles_per_seq = seq // tm
    row = pl.BlockSpec((tm, d), lambda i: (i, 0))
    return pl.pallas_call(
        functools.partial(_post_mixer_kernel, ff_chunk=ff_chunk, final_norm=final_norm),
        grid=(t // tm,),
        in_specs=[
            row,
            pl.BlockSpec((tm, y.shape[1]), lambda i: (i, 0)),
            _resident(w_mix.shape),
            _resident((1, d), layer),
            _resident((d, xa_w), layer),
            pl.BlockSpec((mem_len, 2 * xa_w), lambda i: (i // tiles_per_seq, layer)),
            _resident((xa_w, d), layer),
            _resident((1, d), layer),
            _resident((d, d_ff), layer),
            _resident((d, d_ff), layer),
            _resident((d_ff, d), layer),
            _resident((1, d)),
        ],
        out_specs=row,
        out_shape=jax.ShapeDtypeStruct((t, d), jnp.float32),
        scratch_shapes=[pltpu.VMEM((tm, xa_w), jnp.bfloat16)],
        compiler_params=_params("parallel"),
        name="post_mixer",
    )(x, y, w_mix, xa_g, w_q, kv, w_o, ffn_g, wg, wu, wd, norm_g)


def kernel(x, mem, ffn1_norm, ffn1_w_gate, ffn1_w_up, ffn1_w_down, mix_norm, ev_w_in, ev_conv_w,
           ev_sinks, ev_w_out, od_w_in, od_w_out, rel_bias, xa_norm, xa_w_q, xa_w_kv, xa_w_o,
           mem_norm, ffn2_norm, ffn2_w_gate, ffn2_w_up, ffn2_w_down, final_norm):
    b, s, d = x.shape
    depth = ffn1_norm.shape[0]
    mem_len = mem.shape[1]
    bf16 = jnp.bfloat16
    row = lambda g: g.reshape(1, d)

    kv_w = jnp.concatenate([xa_w_kv[l] for l in range(depth)], axis=1).astype(bf16)
    kv = _norm_proj(mem.reshape(b * mem_len, d), row(mem_norm), kv_w, tm=min(512, b * mem_len))

    gains = lambda g: g.reshape(depth, 1, d)
    ffn1 = (gains(ffn1_norm), ffn1_w_gate.astype(bf16), ffn1_w_up.astype(bf16), ffn1_w_down.astype(bf16))
    ffn2 = (gains(ffn2_norm), ffn2_w_gate.astype(bf16), ffn2_w_up.astype(bf16), ffn2_w_down.astype(bf16))
    xa_q, xa_o = xa_w_q.astype(bf16), xa_w_o.astype(bf16)

    xt = x.reshape(b * s, d)
    for l in range(depth):
        xt = _ffn(xt, *ffn1, l)
        i = l // 2
        if l % 2 == 0:
            v0 = 3 * CONV_CH + (SWA_HEADS + SWA_KV_HEADS) * SWA_HEAD_DIM
            w_in = ev_w_in[i].astype(bf16)
            z, vt = _mixer_proj(xt, row(mix_norm[l]), w_in[:, :v0], w_in[:, v0:].T,
                                head_dim=SWA_HEAD_DIM, sum_rows=SWA_SUM_ROWS, seq=s)
            y = _even_mixer(z.reshape(b, s, -1), vt, ev_conv_w[i], ev_sinks[i], rel_bias)
            w_mix = ev_w_out[i]
        else:
            v0 = (MOBA_HEADS + MOBA_KV_HEADS) * MOBA_HEAD_DIM
            w_in = od_w_in[i].astype(bf16)
            qk, vt = _mixer_proj(xt, row(mix_norm[l]), w_in[:, :v0], w_in[:, v0:].T,
                                 head_dim=MOBA_HEAD_DIM, sum_rows=MOBA_SUM_ROWS, seq=s)
            y = _moba(qk.reshape(b, s, -1), vt, rel_bias)
            w_mix = od_w_out[i]
        xt = _post_mixer(xt, y.reshape(b * s, -1), w_mix.astype(bf16), gains(xa_norm), xa_q, kv, l,
                         xa_o, *ffn2, row(final_norm), final_norm=(l == depth - 1), seq=s,
                         mem_len=mem_len)
    return xt.reshape(b, s, d)
```

```python
import functools
import math

import jax
import jax.numpy as jnp
import numpy as np
from jax import lax
from jax.experimental import pallas as pl
from jax.experimental.pallas import tpu as pltpu

EPS = 1e-6
CONV_WIDTH = 3
CONV_CH = 512
SWA_HEADS = 8
SWA_KV_HEADS = 2
SWA_HEAD_DIM = 64
WINDOW = 128
MOBA_HEADS = 8
MOBA_KV_HEADS = 4
MOBA_HEAD_DIM = 128
MOBA_BLOCK = 256
MOBA_TOPK = 3
MOBA_SUM_ROWS = 16
SWA_SUM_ROWS = 64
REL_BUCKETS = 32
REL_MAX_DIST = 128
XA_HEADS = 4
XA_HEAD_DIM = 128

V7X_VMEM_BYTES = 64 * 1024 * 1024
VMEM_LIMIT_BYTES = V7X_VMEM_BYTES * 3 // 4
V7X_LANES = 128
MASKED = -1e30
LOG2E = math.log2(math.e)

_NT = (((1,), (1,)), ((), ()))


def _params(*semantics):
    return pltpu.CompilerParams(dimension_semantics=semantics, vmem_limit_bytes=VMEM_LIMIT_BYTES)


def _resident(shape, layer=None):
    zeros = (0,) * len(shape)
    if layer is None:
        return pl.BlockSpec(shape, lambda *_: zeros, pipeline_mode=pl.Buffered(1))
    return pl.BlockSpec((None,) + tuple(shape), lambda *_: (layer,) + zeros,
                        pipeline_mode=pl.Buffered(1))


def _rms(x, g):
    return x * lax.rsqrt(jnp.mean(x * x, axis=-1, keepdims=True) + EPS) * g


def _dot(a, b):
    return jnp.dot(a, b, preferred_element_type=jnp.float32)


def _dot_nt(a, b):
    return lax.dot_general(a, b, _NT, preferred_element_type=jnp.float32)


def _half_swiglu(x, g_ref, wg_ref, wu_ref, wd_ref, ff_chunk):
    n = _rms(x, g_ref[...]).astype(jnp.bfloat16)
    d_ff = wg_ref.shape[1]
    acc = jnp.zeros(x.shape, jnp.float32)
    for c in range(d_ff // ff_chunk):
        cols = slice(c * ff_chunk, (c + 1) * ff_chunk)
        gate = _dot(n, wg_ref[:, cols])
        up = _dot(n, wu_ref[:, cols])
        h = (gate * jax.nn.sigmoid(gate) * up).astype(jnp.bfloat16)
        acc = acc + _dot(h, wd_ref[cols, :])
    return x + 0.5 * acc


def _ffn_kernel(x_ref, g_ref, wg_ref, wu_ref, wd_ref, o_ref, *, ff_chunk):
    o_ref[...] = _half_swiglu(x_ref[...], g_ref, wg_ref, wu_ref, wd_ref, ff_chunk)


def _ffn(x, g, wg, wu, wd, layer, *, tm=512, ff_chunk=256):
    t, d = x.shape
    d_ff = wg.shape[2]
    assert t % tm == 0 and d_ff % ff_chunk == 0
    row = pl.BlockSpec((tm, d), lambda i: (i, 0))
    return pl.pallas_call(
        functools.partial(_ffn_kernel, ff_chunk=ff_chunk),
        grid=(t // tm,),
        in_specs=[row, _resident((1, d), layer), _resident((d, d_ff), layer),
                  _resident((d, d_ff), layer), _resident((d_ff, d), layer)],
        out_specs=row,
        out_shape=jax.ShapeDtypeStruct((t, d), jnp.float32),
        compiler_params=_params("parallel"),
        name="ffn",
    )(x, g, wg, wu, wd)


def _norm_proj_kernel(x_ref, g_ref, w_ref, o_ref):
    n = _rms(x_ref[...], g_ref[...]).astype(jnp.bfloat16)
    o_ref[...] = _dot(n, w_ref[...]).astype(o_ref.dtype)


def _norm_proj(x, g, w, *, tm=512):
    t, d = x.shape
    n_out = w.shape[1]
    assert t % tm == 0
    return pl.pallas_call(
        _norm_proj_kernel,
        grid=(t // tm,),
        in_specs=[pl.BlockSpec((tm, d), lambda i: (i, 0)), _resident((1, d)), _resident((d, n_out))],
        out_specs=pl.BlockSpec((tm, n_out), lambda i: (i, 0)),
        out_shape=jax.ShapeDtypeStruct((t, n_out), jnp.bfloat16),
        compiler_params=_params("parallel"),
        name="norm_proj",
    )(x, g, w)


def _mixer_proj_kernel(x_ref, g_ref, w_ref, wvt_ref, z_ref, vt_ref, *, head_dim):
    n = _rms(x_ref[...], g_ref[...]).astype(jnp.bfloat16)
    z_ref[...] = _dot(n, w_ref[...]).astype(z_ref.dtype)
    vt = _dot_nt(wvt_ref[...], n).astype(vt_ref.dtype)
    sum_rows = vt_ref.shape[2] - head_dim
    for h in range(vt_ref.shape[1]):
        vt_ref[0, h, 0:head_dim] = vt[h * head_dim:(h + 1) * head_dim, :]
        vt_ref[0, h, head_dim:] = jnp.ones((sum_rows, vt.shape[1]), vt_ref.dtype)


def _mixer_proj(x, g, w, w_vt, *, head_dim, sum_rows, seq, tm=512):
    t, d = x.shape
    n_out = w.shape[1]
    kv_heads = w_vt.shape[0] // head_dim
    rows = head_dim + sum_rows
    assert seq % tm == 0
    tiles_per_seq = seq // tm
    return pl.pallas_call(
        functools.partial(_mixer_proj_kernel, head_dim=head_dim),
        grid=(t // tm,),
        in_specs=[pl.BlockSpec((tm, d), lambda i: (i, 0)), _resident((1, d)), _resident(w.shape),
                  _resident(w_vt.shape)],
        out_specs=[
            pl.BlockSpec((tm, n_out), lambda i: (i, 0)),
            pl.BlockSpec((1, kv_heads, rows, tm),
                         lambda i: (i // tiles_per_seq, 0, 0, i % tiles_per_seq)),
        ],
        out_shape=[
            jax.ShapeDtypeStruct((t, n_out), jnp.bfloat16),
            jax.ShapeDtypeStruct((t // seq, kv_heads, rows, seq), jnp.bfloat16),
        ],
        compiler_params=_params("parallel"),
        name="mixer_proj",
    )(x, g, w, w_vt)


def _rel_bucket(dist):
    n = np.maximum(dist, 0)
    max_exact = REL_BUCKETS // 2
    nf = np.maximum(n, 1).astype(np.float32)
    scaled = (np.log(nf / np.float32(max_exact)) / np.float32(math.log(REL_MAX_DIST / max_exact))
              * np.float32(REL_BUCKETS - max_exact))
    large = np.minimum(max_exact + scaled.astype(np.int32), REL_BUCKETS - 1)
    return np.where(n < max_exact, n, large).astype(np.int32)


def _bias_tile(table, n_keys, n_queries, offset, visible):
    period = n_keys + n_queries
    j = np.arange(period)
    dist = np.where(j < n_queries, j, j - period) + offset
    vec = jnp.where(visible(dist)[:, None], table[_rel_bucket(dist)] * LOG2E, MASKED).T
    spread = jnp.tile(vec, (1, n_keys))[:, :n_keys * (period - 1)]
    return spread.reshape(-1, n_keys, period - 1)[:, :, :n_queries]


def _even_mixer_kernel(sinks_ref, z_ref, kh_ref, vt_ref, vth_ref, ch_ref, uh_ref, convw_ref,
                       bias_ref, y_ref, *, halo_rows):
    i = pl.program_id(1)
    ts = z_ref.shape[1]
    c1, c2, c3 = CONV_CH, 2 * CONV_CH, 3 * CONV_CH
    q0 = c3
    k0 = q0 + SWA_HEADS * SWA_HEAD_DIM
    group = SWA_HEADS // SWA_KV_HEADS
    dh = SWA_HEAD_DIM
    first = i == 0

    f32 = jnp.float32
    v = z_ref[0, :, c1:c2].astype(f32) * z_ref[0, :, c2:c3].astype(f32)
    vh = ch_ref[0].astype(f32) * uh_ref[0].astype(f32)
    vh = jnp.where(first, 0.0, vh)
    row = lax.broadcasted_iota(jnp.int32, v.shape, 0)
    h1 = vh[halo_rows - 1:halo_rows, :]
    h2 = vh[halo_rows - 2:halo_rows - 1, :]
    v1 = jnp.where(row == 0, h1, pltpu.roll(v, 1, 0))
    v2 = jnp.where(row == 0, h2, jnp.where(row == 1, h1, pltpu.roll(v, 2, 0)))
    conv = v2 * convw_ref[0:1, :] + v1 * convw_ref[1:2, :] + v * convw_ref[2:3, :]
    y_ref[0, :, 0:CONV_CH] = (z_ref[0, :, 0:c1].astype(f32) * conv).astype(y_ref.dtype)

    c_s = dh ** -0.5 * LOG2E
    lanes = group * WINDOW
    lane = lax.broadcasted_iota(jnp.int32, (1, lanes), 1)
    for kh in range(SWA_KV_HEADS):
        sink = jnp.zeros((1, lanes), f32)
        for g in range(group):
            sink = jnp.where(lane // WINDOW == g, sinks_ref[kh * group + g] * LOG2E, sink)
        kcol = slice(k0 + kh * dh, k0 + (kh + 1) * dh)
        for qb in range(ts // WINDOW):
            rows = slice(qb * WINDOW, (qb + 1) * WINDOW)
            if qb == 0:
                kc = jnp.concatenate([kh_ref[0, :, kh * dh:(kh + 1) * dh], z_ref[0, rows, kcol]],
                                     axis=0)
                vt = jnp.concatenate([vth_ref[0, kh], vt_ref[0, kh, :, 0:WINDOW]], axis=1)
                bias = bias_ref[kh, jnp.where(first, 1, 0)]
            else:
                kc = z_ref[0, (qb - 1) * WINDOW:(qb + 1) * WINDOW, kcol]
                vt = vt_ref[0, kh, :, (qb - 1) * WINDOW:(qb + 1) * WINDOW]
                bias = bias_ref[kh, 0]
            qs = jnp.concatenate(
                [z_ref[0, rows, q0 + (kh * group + g) * dh:q0 + (kh * group + g + 1) * dh]
                 for g in range(group)], axis=0)
            e = _dot_nt(kc, qs) * c_s + bias
            m = jnp.maximum(jnp.max(e, axis=0, keepdims=True), sink)
            p = jnp.exp2(e - m).astype(jnp.bfloat16)
            acc = _dot(vt, p)
            denom = acc[dh:dh + 1] + jnp.exp2(sink - m)
            out = (acc / denom).T
            for g in range(group):
                head = kh * group + g
                y_ref[0, rows, CONV_CH + head * dh:CONV_CH + (head + 1) * dh] = (
                    out[g * WINDOW:(g + 1) * WINDOW, 0:dh].astype(y_ref.dtype))


def _even_mixer(z, vt, conv_w, sinks, table, *, ts=512, halo_rows=16):
    b, s, n_in = z.shape
    n_out = CONV_CH + SWA_HEADS * SWA_HEAD_DIM
    group = SWA_HEADS // SWA_KV_HEADS
    assert s % ts == 0 and ts % WINDOW == 0 and vt.shape[2] == V7X_LANES
    banded = _bias_tile(table, 2 * WINDOW, WINDOW, WINDOW, lambda d: (d >= 0) & (d < WINDOW))
    own_only = jnp.where(np.arange(2 * WINDOW)[:, None] >= WINDOW, banded, MASKED)

    def by_kv_head(tile):
        tile = tile.reshape(SWA_KV_HEADS, group, 2 * WINDOW, WINDOW)
        return jnp.moveaxis(tile, 1, 2).reshape(SWA_KV_HEADS, 2 * WINDOW, group * WINDOW)

    bias = jnp.stack([by_kv_head(banded), by_kv_head(own_only)], axis=1)
    k_w = SWA_KV_HEADS * SWA_HEAD_DIM
    k_block = (3 * CONV_CH + SWA_HEADS * SWA_HEAD_DIM) // k_w
    w_per_tile = ts // WINDOW
    h_per_tile = ts // halo_rows
    prev_window = lambda i: jnp.maximum(i * w_per_tile - 1, 0)
    prev_rows = lambda i: jnp.maximum(i * h_per_tile - 1, 0)
    return pl.pallas_call(
        functools.partial(_even_mixer_kernel, halo_rows=halo_rows),
        grid=(b, s // ts),
        in_specs=[
            pl.BlockSpec(memory_space=pltpu.SMEM),
            pl.BlockSpec((1, ts, n_in), lambda bi, i: (bi, i, 0)),
            pl.BlockSpec((1, WINDOW, k_w), lambda bi, i: (bi, prev_window(i), k_block)),
            pl.BlockSpec((1,) + vt.shape[1:3] + (ts,), lambda bi, i: (bi, 0, 0, i)),
            pl.BlockSpec((1,) + vt.shape[1:3] + (WINDOW,), lambda bi, i: (bi, 0, 0, prev_window(i))),
            pl.BlockSpec((1, halo_rows, CONV_CH), lambda bi, i: (bi, prev_rows(i), 1)),
            pl.BlockSpec((1, halo_rows, CONV_CH), lambda bi, i: (bi, prev_rows(i), 2)),
            _resident(conv_w.shape),
            _resident(bias.shape),
        ],
        out_specs=pl.BlockSpec((1, ts, n_out), lambda bi, i: (bi, i, 0)),
        out_shape=jax.ShapeDtypeStruct((b, s, n_out), jnp.bfloat16),
        compiler_params=_params("parallel", "parallel"),
        name="even_mixer",
    )(sinks, z, z, vt, vt, z, z, conv_w, bias)


def _moba_kernel(q_ref, k_ref, vt_ref, bias_ref, o_ref,
                 kaug_ref, kmean_ref, neg_ref, qaug_ref, e_ref,
                 *, q_blocks, chunk_blocks):
    step = pl.program_id(2)
    blk = MOBA_BLOCK
    dh = MOBA_HEAD_DIM
    nblk = k_ref.shape[1] // blk
    group = q_ref.shape[2] // dh
    rows = q_blocks * group * blk
    f32, bf16 = jnp.float32, jnp.bfloat16
    c1 = dh ** -0.5 * LOG2E

    @pl.when(step == 0)
    def _():
        onehot_row = lax.broadcasted_iota(jnp.int32, (blk, dh), 1)
        means = []
        for j in range(nblk):
            kj = k_ref[0, j * blk:(j + 1) * blk, :]
            kaug_ref[j * blk:(j + 1) * blk, 0:dh] = kj
            kaug_ref[j * blk:(j + 1) * blk, dh:2 * dh] = (onehot_row == j).astype(bf16)
            means.append(jnp.mean(kj.astype(f32), axis=0, keepdims=True))
        rest = jnp.concatenate(means, axis=0)
        for part in range(3):
            term = rest.astype(bf16)
            kmean_ref[part * nblk:(part + 1) * nblk, :] = term
            rest = rest - term.astype(f32)
        neg_ref[...] = jnp.zeros(neg_ref.shape, f32)

    q = q_ref[0]
    qs = jnp.concatenate([q[sb * blk:(sb + 1) * blk, g * dh:(g + 1) * dh]
                          for sb in range(q_blocks) for g in range(group)], axis=0)
    qaug_ref[:, 0:dh] = qs
    lane = lax.broadcasted_iota(jnp.int32, (1, rows), 1)
    own = step * q_blocks + lane // (group * blk)

    gate3 = _dot_nt(kmean_ref[...], qs)
    gate = gate3[0:nblk] + gate3[nblk:2 * nblk] + gate3[2 * nblk:3 * nblk]
    blkid = lax.broadcasted_iota(jnp.int32, gate.shape, 0)
    remaining = jnp.where(blkid < own, gate, -jnp.inf)
    keep = blkid >= own
    for _ in range(MOBA_TOPK):
        best = jnp.max(remaining, axis=0, keepdims=True)
        first = jnp.min(jnp.where(remaining == best, blkid, nblk), axis=0, keepdims=True)
        taken = (blkid == first) & (best > -jnp.inf)
        keep = keep | taken
        remaining = jnp.where(taken, -jnp.inf, remaining)
    neg_ref[0:nblk, :] = jnp.where(keep, 0.0, MASKED)
    qaug_ref[:, dh:2 * dh] = neg_ref[...].T.astype(bf16)

    def attend(first_q_block):
        width = group * blk

        def chunks(sb):
            n = first_q_block + sb + 1
            return [(c0, min(chunk_blocks, n - c0)) for c0 in range(0, n, chunk_blocks)]

        def exponents(sb, c0, cb, m):
            lanes = slice(sb * width, (sb + 1) * width)
            s = _dot_nt(kaug_ref[c0 * blk:(c0 + cb) * blk, :], qaug_ref[lanes, :])
            for r in range(cb):
                j = c0 + r
                tile = min(first_q_block + sb - j, 2)
                e = s[r * blk:(r + 1) * blk] * c1 + bias_ref[0, tile]
                e_ref[sb % 2, j * blk:(j + 1) * blk, :] = e
                e_max = jnp.max(e, axis=0, keepdims=True)
                m = e_max if m is None else jnp.maximum(m, e_max)
            return m

        def weighted_values(sb, j, m, acc):
            p = jnp.exp2(e_ref[sb % 2, j * blk:(j + 1) * blk, :] - m).astype(bf16)
            return acc + _dot(vt_ref[0, 0, :, j * blk:(j + 1) * blk], p)

        m = [None] * q_blocks
        for sb in range(q_blocks + 1):
            pass1 = chunks(sb) if sb < q_blocks else []
            pass2 = chunks(sb - 1) if sb > 0 else []
            acc = jnp.zeros((vt_ref.shape[2], width), f32)
            for ci in range(max(len(pass1), len(pass2))):
                if ci < len(pass1):
                    m[sb] = exponents(sb, *pass1[ci], m[sb])
                if ci < len(pass2):
                    c0, cb = pass2[ci]
                    for j in range(c0, c0 + cb):
                        acc = weighted_values(sb - 1, j, m[sb - 1], acc)
            if sb > 0:
                out = (acc[0:dh] / acc[dh:dh + 1]).T.astype(o_ref.dtype)
                rows_out = slice((sb - 1) * blk, sb * blk)
                for g in range(group):
                    o_ref[0, rows_out, g * dh:(g + 1) * dh] = out[g * blk:(g + 1) * blk]

    for case in range(nblk // q_blocks):
        @pl.when(step == case)
        def _(case=case):
            attend(case * q_blocks)


def _moba(qk, vt, table, *, q_blocks=8, chunk_blocks=8):
    b, s, _ = qk.shape
    blk, dh = MOBA_BLOCK, MOBA_HEAD_DIM
    assert s % blk == 0 and dh == V7X_LANES
    nblk = s // blk
    q_blocks = q_blocks or nblk
    assert nblk % q_blocks == 0 and nblk <= V7X_LANES
    group = MOBA_HEADS // MOBA_KV_HEADS
    rows = group * blk

    def tile(blocks_back):
        t = _bias_tile(table, blk, blk, blocks_back * blk, lambda d: d >= 0)
        t = t.reshape(MOBA_KV_HEADS, group, blk, blk)
        return jnp.moveaxis(t, 1, 2).reshape(MOBA_KV_HEADS, blk, rows)

    bias = jnp.stack([tile(0), tile(1), tile(2)], axis=1)
    assert (_rel_bucket(np.arange(blk + 1, 2 * blk * nblk)) == REL_BUCKETS - 1).all()
    k_block0 = MOBA_HEADS
    return pl.pallas_call(
        functools.partial(_moba_kernel, q_blocks=q_blocks, chunk_blocks=chunk_blocks),
        grid=(b, MOBA_KV_HEADS, nblk // q_blocks),
        in_specs=[
            pl.BlockSpec((1, q_blocks * blk, group * dh), lambda bi, h, i: (bi, i, h)),
            pl.BlockSpec((1, s, dh), lambda bi, h, i: (bi, 0, k_block0 + h)),
            pl.BlockSpec((1, 1, vt.shape[2], s), lambda bi, h, i: (bi, h, 0, 0)),
            pl.BlockSpec((1, 3, blk, rows), lambda bi, h, i: (h, 0, 0, 0)),
        ],
        out_specs=pl.BlockSpec((1, q_blocks * blk, group * dh), lambda bi, h, i: (bi, i, h)),
        out_shape=jax.ShapeDtypeStruct((b, s, MOBA_HEADS * dh), jnp.bfloat16),
        scratch_shapes=[
            pltpu.VMEM((s, 2 * dh), jnp.bfloat16),
            pltpu.VMEM((3 * nblk, dh), jnp.bfloat16),
            pltpu.VMEM((V7X_LANES, q_blocks * rows), jnp.float32),
            pltpu.VMEM((q_blocks * rows, 2 * dh), jnp.bfloat16),
            pltpu.VMEM((2, s, rows), jnp.float32),
        ],
        compiler_params=_params("parallel", "parallel", "arbitrary"),
        name="moba",
    )(qk, qk, vt, bias)


def _cross_attention(x, g_ref, wq_ref, kv_ref, wo_ref, a_ref):
    dh = XA_HEAD_DIM
    n = _rms(x, g_ref[...]).astype(jnp.bfloat16)
    q = _dot(n, wq_ref[...]).astype(jnp.bfloat16)
    scale = dh ** -0.5
    v0 = XA_HEADS * dh
    for h in range(XA_HEADS):
        cols = slice(h * dh, (h + 1) * dh)
        s = _dot_nt(q[:, cols], kv_ref[:, cols]) * scale
        m = jnp.max(s, axis=-1, keepdims=True)
        p = jnp.exp(s - m)
        l = jnp.sum(p, axis=-1, keepdims=True)
        out = _dot(p.astype(jnp.bfloat16), kv_ref[:, v0 + h * dh:v0 + (h + 1) * dh]) / l
        a_ref[:, cols] = out.astype(a_ref.dtype)
    return x + _dot(a_ref[...], wo_ref[...])


def _post_mixer_kernel(x_ref, y_ref, wmix_ref, xg_ref, wq_ref, kv_ref, wo_ref, fg_ref, wg_ref,
                       wu_ref, wd_ref, ng_ref, o_ref, a_ref, *, ff_chunk, final_norm):
    x = x_ref[...] + _dot(y_ref[...], wmix_ref[...])
    x = _cross_attention(x, xg_ref, wq_ref, kv_ref, wo_ref, a_ref)
    x = _half_swiglu(x, fg_ref, wg_ref, wu_ref, wd_ref, ff_chunk)
    if final_norm:
        x = _rms(x, ng_ref[...])
    o_ref[...] = x


def _post_mixer(x, y, w_mix, xa_g, w_q, kv, layer, w_o, ffn_g, wg, wu, wd, norm_g, *, final_norm,
                seq, mem_len, tm=512, ff_chunk=256):
    t, d = x.shape
    d_ff = wg.shape[2]
    xa_w = XA_HEADS * XA_HEAD_DIM
    assert seq % tm == 0 and d_ff % ff_chunk == 0
    tiles_per_seq = seq // tm
    row = pl.BlockSpec((tm, d), lambda i: (i, 0))
    return pl.pallas_call(
        functools.partial(_post_mixer_kernel, ff_chunk=ff_chunk, final_norm=final_norm),
        grid=(t // tm,),
        in_specs=[
            row,
            pl.BlockSpec((tm, y.shape[1]), lambda i: (i, 0)),
            _resident(w_mix.shape),
            _resident((1, d), layer),
            _resident((d, xa_w), layer),
            pl.BlockSpec((mem_len, 2 * xa_w), lambda i: (i // tiles_per_seq, layer)),
            _resident((xa_w, d), layer),
            _resident((1, d), layer),
            _resident((d, d_ff), layer),
            _resident((d, d_ff), layer),
            _resident((d_ff, d), layer),
            _resident((1, d)),
        ],
        out_specs=row,
        out_shape=jax.ShapeDtypeStruct((t, d), jnp.float32),
        scratch_shapes=[pltpu.VMEM((tm, xa_w), jnp.bfloat16)],
        compiler_params=_params("parallel"),
        name="post_mixer",
    )(x, y, w_mix, xa_g, w_q, kv, w_o, ffn_g, wg, wu, wd, norm_g)


def kernel(x, mem, ffn1_norm, ffn1_w_gate, ffn1_w_up, ffn1_w_down, mix_norm, ev_w_in, ev_conv_w,
           ev_sinks, ev_w_out, od_w_in, od_w_out, rel_bias, xa_norm, xa_w_q, xa_w_kv, xa_w_o,
           mem_norm, ffn2_norm, ffn2_w_gate, ffn2_w_up, ffn2_w_down, final_norm):
    b, s, d = x.shape
    depth = ffn1_norm.shape[0]
    mem_len = mem.shape[1]
    bf16 = jnp.bfloat16
    row = lambda g: g.reshape(1, d)

    kv_w = jnp.concatenate([xa_w_kv[l] for l in range(depth)], axis=1).astype(bf16)
    kv = _norm_proj(mem.reshape(b * mem_len, d), row(mem_norm), kv_w, tm=min(512, b * mem_len))

    gains = lambda g: g.reshape(depth, 1, d)
    ffn1 = (gains(ffn1_norm), ffn1_w_gate.astype(bf16), ffn1_w_up.astype(bf16), ffn1_w_down.astype(bf16))
    ffn2 = (gains(ffn2_norm), ffn2_w_gate.astype(bf16), ffn2_w_up.astype(bf16), ffn2_w_down.astype(bf16))
    xa_q, xa_o = xa_w_q.astype(bf16), xa_w_o.astype(bf16)

    xt = x.reshape(b * s, d)
    for l in range(depth):
        xt = _ffn(xt, *ffn1, l)
        i = l // 2
        if l % 2 == 0:
            v0 = 3 * CONV_CH + (SWA_HEADS + SWA_KV_HEADS) * SWA_HEAD_DIM
            w_in = ev_w_in[i].astype(bf16)
            z, vt = _mixer_proj(xt, row(mix_norm[l]), w_in[:, :v0], w_in[:, v0:].T,
                                head_dim=SWA_HEAD_DIM, sum_rows=SWA_SUM_ROWS, seq=s)
            y = _even_mixer(z.reshape(b, s, -1), vt, ev_conv_w[i], ev_sinks[i], rel_bias)
            w_mix = ev_w_out[i]
        else:
            v0 = (MOBA_HEADS + MOBA_KV_HEADS) * MOBA_HEAD_DIM
            w_in = od_w_in[i].astype(bf16)
            qk, vt = _mixer_proj(xt, row(mix_norm[l]), w_in[:, :v0], w_in[:, v0:].T,
                                 head_dim=MOBA_HEAD_DIM, sum_rows=MOBA_SUM_ROWS, seq=s)
            y = _moba(qk.reshape(b, s, -1), vt, rel_bias)
            w_mix = od_w_out[i]
        xt = _post_mixer(xt, y.reshape(b * s, -1), w_mix.astype(bf16), gains(xa_norm), xa_q, kv, l,
                         xa_o, *ffn2, row(final_norm), final_norm=(l == depth - 1), seq=s,
                         mem_len=mem_len)
    return xt.reshape(b, s, d)
```

```python
import functools
import math

import jax
import jax.numpy as jnp
import numpy as np
from jax import lax
from jax.experimental import pallas as pl
from jax.experimental.pallas import tpu as pltpu

EPS = 1e-6
CONV_WIDTH = 3
CONV_CH = 512
SWA_HEADS = 8
SWA_KV_HEADS = 2
SWA_HEAD_DIM = 64
WINDOW = 128
MOBA_HEADS = 8
MOBA_KV_HEADS = 4
MOBA_HEAD_DIM = 128
MOBA_BLOCK = 256
MOBA_TOPK = 3
MOBA_SUM_ROWS = 16
SWA_SUM_ROWS = 64
REL_BUCKETS = 32
REL_MAX_DIST = 128
XA_HEADS = 4
XA_HEAD_DIM = 128

V7X_VMEM_BYTES = 64 * 1024 * 1024
VMEM_LIMIT_BYTES = V7X_VMEM_BYTES * 3 // 4
V7X_LANES = 128
MASKED = -1e30
LOG2E = math.log2(math.e)

_NT = (((1,), (1,)), ((), ()))


def _params(*semantics, resident_bytes=0):
    limit = min(max(VMEM_LIMIT_BYTES, resident_bytes + V7X_VMEM_BYTES // 4), V7X_VMEM_BYTES * 7 // 8)
    return pltpu.CompilerParams(dimension_semantics=semantics, vmem_limit_bytes=limit)


def _resident(shape, layer=None):
    zeros = (0,) * len(shape)
    if layer is None:
        return pl.BlockSpec(shape, lambda *_: zeros, pipeline_mode=pl.Buffered(1))
    return pl.BlockSpec((None,) + tuple(shape), lambda *_: (layer,) + zeros,
                        pipeline_mode=pl.Buffered(1))


def _rms(x, g):
    return x * lax.rsqrt(jnp.mean(x * x, axis=-1, keepdims=True) + EPS) * g


def _dot(a, b):
    return jnp.dot(a, b, preferred_element_type=jnp.float32)


def _dot_nt(a, b):
    return lax.dot_general(a, b, _NT, preferred_element_type=jnp.float32)


def _half_swiglu(x, g_ref, wg_ref, wu_ref, wd_ref, ff_chunk):
    bf16 = jnp.bfloat16
    n = _rms(x, g_ref[...]).astype(bf16)
    d_ff = wg_ref.shape[1]
    acc = jnp.zeros(x.shape, jnp.float32)
    for c in range(d_ff // ff_chunk):
        cols = slice(c * ff_chunk, (c + 1) * ff_chunk)
        gate = _dot(n, wg_ref[:, cols].astype(bf16))
        up = _dot(n, wu_ref[:, cols].astype(bf16))
        h = (gate * jax.nn.sigmoid(gate) * up).astype(bf16)
        acc = acc + _dot(h, wd_ref[cols, :].astype(bf16))
    return x + 0.5 * acc


def _ffn_kernel(x_ref, g_ref, wg_ref, wu_ref, wd_ref, o_ref, *, ff_chunk):
    o_ref[...] = _half_swiglu(x_ref[...], g_ref, wg_ref, wu_ref, wd_ref, ff_chunk)


def _ffn(x, g, wg, wu, wd, layer, *, tm=512, ff_chunk=256):
    t, d = x.shape
    d_ff = wg.shape[2]
    assert t % tm == 0 and d_ff % ff_chunk == 0
    row = pl.BlockSpec((tm, d), lambda i: (i, 0))
    weight_bytes = sum(w[layer].size * w.dtype.itemsize for w in (wg, wu, wd))
    return pl.pallas_call(
        functools.partial(_ffn_kernel, ff_chunk=ff_chunk),
        grid=(t // tm,),
        in_specs=[row, _resident((1, d), layer), _resident((d, d_ff), layer),
                  _resident((d, d_ff), layer), _resident((d_ff, d), layer)],
        out_specs=row,
        out_shape=jax.ShapeDtypeStruct((t, d), jnp.float32),
        compiler_params=_params("parallel", resident_bytes=weight_bytes),
        name="ffn",
    )(x, g, wg, wu, wd)


def _norm_proj_kernel(x_ref, g_ref, w_ref, o_ref):
    n = _rms(x_ref[...], g_ref[...]).astype(jnp.bfloat16)
    o_ref[...] = _dot(n, w_ref[...]).astype(o_ref.dtype)


def _norm_proj(x, g, w, *, tm=512):
    t, d = x.shape
    n_out = w.shape[1]
    assert t % tm == 0
    return pl.pallas_call(
        _norm_proj_kernel,
        grid=(t // tm,),
        in_specs=[pl.BlockSpec((tm, d), lambda i: (i, 0)), _resident((1, d)), _resident((d, n_out))],
        out_specs=pl.BlockSpec((tm, n_out), lambda i: (i, 0)),
        out_shape=jax.ShapeDtypeStruct((t, n_out), jnp.bfloat16),
        compiler_params=_params("parallel"),
        name="norm_proj",
    )(x, g, w)


def _mixer_proj_kernel(x_ref, g_ref, w_ref, wvt_ref, z_ref, vt_ref, *, head_dim):
    n = _rms(x_ref[...], g_ref[...]).astype(jnp.bfloat16)
    z_ref[...] = _dot(n, w_ref[...]).astype(z_ref.dtype)
    vt = _dot_nt(wvt_ref[...], n).astype(vt_ref.dtype)
    sum_rows = vt_ref.shape[2] - head_dim
    for h in range(vt_ref.shape[1]):
        vt_ref[0, h, 0:head_dim] = vt[h * head_dim:(h + 1) * head_dim, :]
        vt_ref[0, h, head_dim:] = jnp.ones((sum_rows, vt.shape[1]), vt_ref.dtype)


def _mixer_proj(x, g, w, w_vt, *, head_dim, sum_rows, seq, tm=512):
    t, d = x.shape
    n_out = w.shape[1]
    kv_heads = w_vt.shape[0] // head_dim
    rows = head_dim + sum_rows
    assert seq % tm == 0
    tiles_per_seq = seq // tm
    return pl.pallas_call(
        functools.partial(_mixer_proj_kernel, head_dim=head_dim),
        grid=(t // tm,),
        in_specs=[pl.BlockSpec((tm, d), lambda i: (i, 0)), _resident((1, d)), _resident(w.shape),
                  _resident(w_vt.shape)],
        out_specs=[
            pl.BlockSpec((tm, n_out), lambda i: (i, 0)),
            pl.BlockSpec((1, kv_heads, rows, tm),
                         lambda i: (i // tiles_per_seq, 0, 0, i % tiles_per_seq)),
        ],
        out_shape=[
            jax.ShapeDtypeStruct((t, n_out), jnp.bfloat16),
            jax.ShapeDtypeStruct((t // seq, kv_heads, rows, seq), jnp.bfloat16),
        ],
        compiler_params=_params("parallel"),
        name="mixer_proj",
    )(x, g, w, w_vt)


def _rel_bucket(dist):
    n = np.maximum(dist, 0)
    max_exact = REL_BUCKETS // 2
    nf = np.maximum(n, 1).astype(np.float32)
    scaled = (np.log(nf / np.float32(max_exact)) / np.float32(math.log(REL_MAX_DIST / max_exact))
              * np.float32(REL_BUCKETS - max_exact))
    large = np.minimum(max_exact + scaled.astype(np.int32), REL_BUCKETS - 1)
    return np.where(n < max_exact, n, large).astype(np.int32)


def _bias_tile(table, n_keys, n_queries, offset, visible):
    period = n_keys + n_queries
    j = np.arange(period)
    dist = np.where(j < n_queries, j, j - period) + offset
    vec = jnp.where(visible(dist)[:, None], table[_rel_bucket(dist)] * LOG2E, MASKED).T
    spread = jnp.tile(vec, (1, n_keys))[:, :n_keys * (period - 1)]
    return spread.reshape(-1, n_keys, period - 1)[:, :, :n_queries]


def _even_mixer_kernel(sinks_ref, z_ref, kh_ref, vt_ref, vth_ref, ch_ref, uh_ref, convw_ref,
                       bias_ref, y_ref, *, halo_rows):
    i = pl.program_id(1)
    ts = z_ref.shape[1]
    c1, c2, c3 = CONV_CH, 2 * CONV_CH, 3 * CONV_CH
    q0 = c3
    k0 = q0 + SWA_HEADS * SWA_HEAD_DIM
    group = SWA_HEADS // SWA_KV_HEADS
    dh = SWA_HEAD_DIM
    first = i == 0

    f32 = jnp.float32
    v = z_ref[0, :, c1:c2].astype(f32) * z_ref[0, :, c2:c3].astype(f32)
    vh = ch_ref[0].astype(f32) * uh_ref[0].astype(f32)
    vh = jnp.where(first, 0.0, vh)
    row = lax.broadcasted_iota(jnp.int32, v.shape, 0)
    h1 = vh[halo_rows - 1:halo_rows, :]
    h2 = vh[halo_rows - 2:halo_rows - 1, :]
    v1 = jnp.where(row == 0, h1, pltpu.roll(v, 1, 0))
    v2 = jnp.where(row == 0, h2, jnp.where(row == 1, h1, pltpu.roll(v, 2, 0)))
    conv = v2 * convw_ref[0:1, :] + v1 * convw_ref[1:2, :] + v * convw_ref[2:3, :]
    y_ref[0, :, 0:CONV_CH] = (z_ref[0, :, 0:c1].astype(f32) * conv).astype(y_ref.dtype)

    c_s = dh ** -0.5 * LOG2E
    lanes = group * WINDOW
    lane = lax.broadcasted_iota(jnp.int32, (1, lanes), 1)
    for kh in range(SWA_KV_HEADS):
        sink = jnp.zeros((1, lanes), f32)
        for g in range(group):
            sink = jnp.where(lane // WINDOW == g, sinks_ref[kh * group + g] * LOG2E, sink)
        kcol = slice(k0 + kh * dh, k0 + (kh + 1) * dh)
        for qb in range(ts // WINDOW):
            rows = slice(qb * WINDOW, (qb + 1) * WINDOW)
            if qb == 0:
                kc = jnp.concatenate([kh_ref[0, :, kh * dh:(kh + 1) * dh], z_ref[0, rows, kcol]],
                                     axis=0)
                vt = jnp.concatenate([vth_ref[0, kh], vt_ref[0, kh, :, 0:WINDOW]], axis=1)
                bias = bias_ref[kh, jnp.where(first, 1, 0)]
            else:
                kc = z_ref[0, (qb - 1) * WINDOW:(qb + 1) * WINDOW, kcol]
                vt = vt_ref[0, kh, :, (qb - 1) * WINDOW:(qb + 1) * WINDOW]
                bias = bias_ref[kh, 0]
            qs = jnp.concatenate(
                [z_ref[0, rows, q0 + (kh * group + g) * dh:q0 + (kh * group + g + 1) * dh]
                 for g in range(group)], axis=0)
            e = _dot_nt(kc, qs) * c_s + bias
            m = jnp.maximum(jnp.max(e, axis=0, keepdims=True), sink)
            p = jnp.exp2(e - m).astype(jnp.bfloat16)
            acc = _dot(vt, p)
            denom = acc[dh:dh + 1] + jnp.exp2(sink - m)
            out = (acc / denom).T
            for g in range(group):
                head = kh * group + g
                y_ref[0, rows, CONV_CH + head * dh:CONV_CH + (head + 1) * dh] = (
                    out[g * WINDOW:(g + 1) * WINDOW, 0:dh].astype(y_ref.dtype))


def _even_mixer(z, vt, conv_w, sinks, table, *, ts=512, halo_rows=16):
    b, s, n_in = z.shape
    n_out = CONV_CH + SWA_HEADS * SWA_HEAD_DIM
    group = SWA_HEADS // SWA_KV_HEADS
    assert s % ts == 0 and ts % WINDOW == 0 and vt.shape[2] == V7X_LANES
    banded = _bias_tile(table, 2 * WINDOW, WINDOW, WINDOW, lambda d: (d >= 0) & (d < WINDOW))
    own_only = jnp.where(np.arange(2 * WINDOW)[:, None] >= WINDOW, banded, MASKED)

    def by_kv_head(tile):
        tile = tile.reshape(SWA_KV_HEADS, group, 2 * WINDOW, WINDOW)
        return jnp.moveaxis(tile, 1, 2).reshape(SWA_KV_HEADS, 2 * WINDOW, group * WINDOW)

    bias = jnp.stack([by_kv_head(banded), by_kv_head(own_only)], axis=1)
    k_w = SWA_KV_HEADS * SWA_HEAD_DIM
    k_block = (3 * CONV_CH + SWA_HEADS * SWA_HEAD_DIM) // k_w
    w_per_tile = ts // WINDOW
    h_per_tile = ts // halo_rows
    prev_window = lambda i: jnp.maximum(i * w_per_tile - 1, 0)
    prev_rows = lambda i: jnp.maximum(i * h_per_tile - 1, 0)
    return pl.pallas_call(
        functools.partial(_even_mixer_kernel, halo_rows=halo_rows),
        grid=(b, s // ts),
        in_specs=[
            pl.BlockSpec(memory_space=pltpu.SMEM),
            pl.BlockSpec((1, ts, n_in), lambda bi, i: (bi, i, 0)),
            pl.BlockSpec((1, WINDOW, k_w), lambda bi, i: (bi, prev_window(i), k_block)),
            pl.BlockSpec((1,) + vt.shape[1:3] + (ts,), lambda bi, i: (bi, 0, 0, i)),
            pl.BlockSpec((1,) + vt.shape[1:3] + (WINDOW,), lambda bi, i: (bi, 0, 0, prev_window(i))),
            pl.BlockSpec((1, halo_rows, CONV_CH), lambda bi, i: (bi, prev_rows(i), 1)),
            pl.BlockSpec((1, halo_rows, CONV_CH), lambda bi, i: (bi, prev_rows(i), 2)),
            _resident(conv_w.shape),
            _resident(bias.shape),
        ],
        out_specs=pl.BlockSpec((1, ts, n_out), lambda bi, i: (bi, i, 0)),
        out_shape=jax.ShapeDtypeStruct((b, s, n_out), jnp.bfloat16),
        compiler_params=_params("parallel", "parallel"),
        name="even_mixer",
    )(sinks, z, z, vt, vt, z, z, conv_w, bias)


def _moba_kernel(q_ref, k_ref, vt_ref, bias_ref, o_ref,
                 kaug_ref, kmean_ref, neg_ref, qaug_ref, e_ref,
                 *, q_blocks, chunk_blocks):
    step = pl.program_id(2)
    blk = MOBA_BLOCK
    dh = MOBA_HEAD_DIM
    nblk = k_ref.shape[1] // blk
    group = q_ref.shape[2] // dh
    rows = q_blocks * group * blk
    f32, bf16 = jnp.float32, jnp.bfloat16
    c1 = dh ** -0.5 * LOG2E

    @pl.when(step == 0)
    def _():
        onehot_row = lax.broadcasted_iota(jnp.int32, (blk, dh), 1)
        means = []
        for j in range(nblk):
            kj = k_ref[0, j * blk:(j + 1) * blk, :]
            kaug_ref[j * blk:(j + 1) * blk, 0:dh] = kj
            kaug_ref[j * blk:(j + 1) * blk, dh:2 * dh] = (onehot_row == j).astype(bf16)
            means.append(jnp.mean(kj.astype(f32), axis=0, keepdims=True))
        rest = jnp.concatenate(means, axis=0)
        for part in range(3):
            term = rest.astype(bf16)
            kmean_ref[part * nblk:(part + 1) * nblk, :] = term
            rest = rest - term.astype(f32)
        neg_ref[...] = jnp.zeros(neg_ref.shape, f32)

    q = q_ref[0]
    qs = jnp.concatenate([q[sb * blk:(sb + 1) * blk, g * dh:(g + 1) * dh]
                          for sb in range(q_blocks) for g in range(group)], axis=0)
    qaug_ref[:, 0:dh] = qs
    lane = lax.broadcasted_iota(jnp.int32, (1, rows), 1)
    own = step * q_blocks + lane // (group * blk)

    gate3 = _dot_nt(kmean_ref[...], qs)
    gate = gate3[0:nblk] + gate3[nblk:2 * nblk] + gate3[2 * nblk:3 * nblk]
    blkid = lax.broadcasted_iota(jnp.int32, gate.shape, 0)
    remaining = jnp.where(blkid < own, gate, -jnp.inf)
    keep = blkid >= own
    for _ in range(MOBA_TOPK):
        best = jnp.max(remaining, axis=0, keepdims=True)
        first = jnp.min(jnp.where(remaining == best, blkid, nblk), axis=0, keepdims=True)
        taken = (blkid == first) & (best > -jnp.inf)
        keep = keep | taken
        remaining = jnp.where(taken, -jnp.inf, remaining)
    neg_ref[0:nblk, :] = jnp.where(keep, 0.0, MASKED)
    qaug_ref[:, dh:2 * dh] = neg_ref[...].T.astype(bf16)

    def attend(first_q_block):
        width = group * blk

        def chunks(sb):
            n = first_q_block + sb + 1
            return [(c0, min(chunk_blocks, n - c0)) for c0 in range(0, n, chunk_blocks)]

        def exponents(sb, c0, cb, m):
            lanes = slice(sb * width, (sb + 1) * width)
            s = _dot_nt(kaug_ref[c0 * blk:(c0 + cb) * blk, :], qaug_ref[lanes, :])
            for r in range(cb):
                j = c0 + r
                tile = min(first_q_block + sb - j, 2)
                e = s[r * blk:(r + 1) * blk] * c1 + bias_ref[0, tile]
                e_ref[sb % 2, j * blk:(j + 1) * blk, :] = e
                e_max = jnp.max(e, axis=0, keepdims=True)
                m = e_max if m is None else jnp.maximum(m, e_max)
            return m

        def weighted_values(sb, j, m, acc):
            p = jnp.exp2(e_ref[sb % 2, j * blk:(j + 1) * blk, :] - m).astype(bf16)
            return acc + _dot(vt_ref[0, 0, :, j * blk:(j + 1) * blk], p)

        m = [None] * q_blocks
        for sb in range(q_blocks + 1):
            pass1 = chunks(sb) if sb < q_blocks else []
            pass2 = chunks(sb - 1) if sb > 0 else []
            acc = jnp.zeros((vt_ref.shape[2], width), f32)
            for ci in range(max(len(pass1), len(pass2))):
                if ci < len(pass1):
                    m[sb] = exponents(sb, *pass1[ci], m[sb])
                if ci < len(pass2):
                    c0, cb = pass2[ci]
                    for j in range(c0, c0 + cb):
                        acc = weighted_values(sb - 1, j, m[sb - 1], acc)
            if sb > 0:
                out = (acc[0:dh] / acc[dh:dh + 1]).T.astype(o_ref.dtype)
                rows_out = slice((sb - 1) * blk, sb * blk)
                for g in range(group):
                    o_ref[0, rows_out, g * dh:(g + 1) * dh] = out[g * blk:(g + 1) * blk]

    for case in range(nblk // q_blocks):
        @pl.when(step == case)
        def _(case=case):
            attend(case * q_blocks)


def _moba(qk, vt, table, *, q_blocks=8, chunk_blocks=8):
    b, s, _ = qk.shape
    blk, dh = MOBA_BLOCK, MOBA_HEAD_DIM
    assert s % blk == 0 and dh == V7X_LANES
    nblk = s // blk
    q_blocks = q_blocks or nblk
    assert nblk % q_blocks == 0 and nblk <= V7X_LANES
    group = MOBA_HEADS // MOBA_KV_HEADS
    rows = group * blk

    def tile(blocks_back):
        t = _bias_tile(table, blk, blk, blocks_back * blk, lambda d: d >= 0)
        t = t.reshape(MOBA_KV_HEADS, group, blk, blk)
        return jnp.moveaxis(t, 1, 2).reshape(MOBA_KV_HEADS, blk, rows)

    bias = jnp.stack([tile(0), tile(1), tile(2)], axis=1)
    assert (_rel_bucket(np.arange(blk + 1, 2 * blk * nblk)) == REL_BUCKETS - 1).all()
    k_block0 = MOBA_HEADS
    return pl.pallas_call(
        functools.partial(_moba_kernel, q_blocks=q_blocks, chunk_blocks=chunk_blocks),
        grid=(b, MOBA_KV_HEADS, nblk // q_blocks),
        in_specs=[
            pl.BlockSpec((1, q_blocks * blk, group * dh), lambda bi, h, i: (bi, i, h)),
            pl.BlockSpec((1, s, dh), lambda bi, h, i: (bi, 0, k_block0 + h)),
            pl.BlockSpec((1, 1, vt.shape[2], s), lambda bi, h, i: (bi, h, 0, 0)),
            pl.BlockSpec((1, 3, blk, rows), lambda bi, h, i: (h, 0, 0, 0)),
        ],
        out_specs=pl.BlockSpec((1, q_blocks * blk, group * dh), lambda bi, h, i: (bi, i, h)),
        out_shape=jax.ShapeDtypeStruct((b, s, MOBA_HEADS * dh), jnp.bfloat16),
        scratch_shapes=[
            pltpu.VMEM((s, 2 * dh), jnp.bfloat16),
            pltpu.VMEM((3 * nblk, dh), jnp.bfloat16),
            pltpu.VMEM((V7X_LANES, q_blocks * rows), jnp.float32),
            pltpu.VMEM((q_blocks * rows, 2 * dh), jnp.bfloat16),
            pltpu.VMEM((2, s, rows), jnp.float32),
        ],
        compiler_params=_params("parallel", "parallel", "arbitrary"),
        name="moba",
    )(qk, qk, vt, bias)


def _cross_attention(x, g_ref, wq_ref, kv_ref, wo_ref, a_ref):
    dh = XA_HEAD_DIM
    n = _rms(x, g_ref[...]).astype(jnp.bfloat16)
    q = _dot(n, wq_ref[...]).astype(jnp.bfloat16)
    scale = dh ** -0.5
    v0 = XA_HEADS * dh
    for h in range(XA_HEADS):
        cols = slice(h * dh, (h + 1) * dh)
        s = _dot_nt(q[:, cols], kv_ref[:, cols]) * scale
        m = jnp.max(s, axis=-1, keepdims=True)
        p = jnp.exp(s - m)
        l = jnp.sum(p, axis=-1, keepdims=True)
        out = _dot(p.astype(jnp.bfloat16), kv_ref[:, v0 + h * dh:v0 + (h + 1) * dh]) / l
        a_ref[:, cols] = out.astype(a_ref.dtype)
    return x + _dot(a_ref[...], wo_ref[...])


def _post_mixer_kernel(x_ref, y_ref, wmix_ref, xg_ref, wq_ref, kv_ref, wo_ref, fg_ref, wg_ref,
                       wu_ref, wd_ref, ng_ref, o_ref, a_ref, *, ff_chunk, final_norm):
    x = x_ref[...] + _dot(y_ref[...], wmix_ref[...])
    x = _cross_attention(x, xg_ref, wq_ref, kv_ref, wo_ref, a_ref)
    x = _half_swiglu(x, fg_ref, wg_ref, wu_ref, wd_ref, ff_chunk)
    if final_norm:
        x = _rms(x, ng_ref[...])
    o_ref[...] = x


def _post_mixer(x, y, w_mix, xa_g, w_q, kv, layer, w_o, ffn_g, wg, wu, wd, norm_g, *, final_norm,
                seq, mem_len, tm=512, ff_chunk=256):
    t, d = x.shape
    d_ff = wg.shape[2]
    xa_w = XA_HEADS * XA_HEAD_DIM
    assert seq % tm == 0 and d_ff % ff_chunk == 0
    tiles_per_seq = seq // tm
    row = pl.BlockSpec((tm, d), lambda i: (i, 0))
    weight_bytes = (w_mix.size * w_mix.dtype.itemsize
                    + sum(w[layer].size * w.dtype.itemsize for w in (w_q, w_o, wg, wu, wd)))
    return pl.pallas_call(
        functools.partial(_post_mixer_kernel, ff_chunk=ff_chunk, final_norm=final_norm),
        grid=(t // tm,),
        in_specs=[
            row,
            pl.BlockSpec((tm, y.shape[1]), lambda i: (i, 0)),
            _resident(w_mix.shape),
            _resident((1, d), layer),
            _resident((d, xa_w), layer),
            pl.BlockSpec((mem_len, 2 * xa_w), lambda i: (i // tiles_per_seq, layer)),
            _resident((xa_w, d), layer),
            _resident((1, d), layer),
            _resident((d, d_ff), layer),
            _resident((d, d_ff), layer),
            _resident((d_ff, d), layer),
            _resident((1, d)),
        ],
        out_specs=row,
        out_shape=jax.ShapeDtypeStruct((t, d), jnp.float32),
        scratch_shapes=[pltpu.VMEM((tm, xa_w), jnp.bfloat16)],
        compiler_params=_params("parallel", resident_bytes=weight_bytes),
        name="post_mixer",
    )(x, y, w_mix, xa_g, w_q, kv, w_o, ffn_g, wg, wu, wd, norm_g)


def kernel(x, mem, ffn1_norm, ffn1_w_gate, ffn1_w_up, ffn1_w_down, mix_norm, ev_w_in, ev_conv_w,
           ev_sinks, ev_w_out, od_w_in, od_w_out, rel_bias, xa_norm, xa_w_q, xa_w_kv, xa_w_o,
           mem_norm, ffn2_norm, ffn2_w_gate, ffn2_w_up, ffn2_w_down, final_norm):
    b, s, d = x.shape
    depth = ffn1_norm.shape[0]
    mem_len = mem.shape[1]
    bf16 = jnp.bfloat16
    row = lambda g: g.reshape(1, d)

    kv_w = jnp.concatenate([xa_w_kv[l] for l in range(depth)], axis=1).astype(bf16)
    kv = _norm_proj(mem.reshape(b * mem_len, d), row(mem_norm), kv_w, tm=min(512, b * mem_len))

    gains = lambda g: g.reshape(depth, 1, d)
    ffn1 = (gains(ffn1_norm), ffn1_w_gate, ffn1_w_up, ffn1_w_down)
    ffn2 = (gains(ffn2_norm), ffn2_w_gate, ffn2_w_up, ffn2_w_down)
    xa_q, xa_o = xa_w_q.astype(bf16), xa_w_o.astype(bf16)

    xt = x.reshape(b * s, d)
    for l in range(depth):
        xt = _ffn(xt, *ffn1, l)
        i = l // 2
        if l % 2 == 0:
            v0 = 3 * CONV_CH + (SWA_HEADS + SWA_KV_HEADS) * SWA_HEAD_DIM
            w_in = ev_w_in[i].astype(bf16)
            z, vt = _mixer_proj(xt, row(mix_norm[l]), w_in[:, :v0], w_in[:, v0:].T,
                                head_dim=SWA_HEAD_DIM, sum_rows=SWA_SUM_ROWS, seq=s)
            y = _even_mixer(z.reshape(b, s, -1), vt, ev_conv_w[i], ev_sinks[i], rel_bias)
            w_mix = ev_w_out[i]
        else:
            v0 = (MOBA_HEADS + MOBA_KV_HEADS) * MOBA_HEAD_DIM
            w_in = od_w_in[i].astype(bf16)
            qk, vt = _mixer_proj(xt, row(mix_norm[l]), w_in[:, :v0], w_in[:, v0:].T,
                                 head_dim=MOBA_HEAD_DIM, sum_rows=MOBA_SUM_ROWS, seq=s)
            y = _moba(qk.reshape(b, s, -1), vt, rel_bias)
            w_mix = od_w_out[i]
        xt = _post_mixer(xt, y.reshape(b * s, -1), w_mix.astype(bf16), gains(xa_norm), xa_q, kv, l,
                         xa_o, *ffn2, row(final_norm), final_norm=(l == depth - 1), seq=s,
                         mem_len=mem_len)
    return xt.reshape(b, s, d)
```

```python
import functools
import math

import jax
import jax.numpy as jnp
import numpy as np
from jax import lax
from jax.experimental import pallas as pl
from jax.experimental.pallas import tpu as pltpu

EPS = 1e-6
CONV_WIDTH = 3
CONV_CH = 512
SWA_HEADS = 8
SWA_KV_HEADS = 2
SWA_HEAD_DIM = 64
WINDOW = 128
MOBA_HEADS = 8
MOBA_KV_HEADS = 4
MOBA_HEAD_DIM = 128
MOBA_BLOCK = 256
MOBA_TOPK = 3
MOBA_SUM_ROWS = 16
SWA_SUM_ROWS = 64
REL_BUCKETS = 32
REL_MAX_DIST = 128
XA_HEADS = 4
XA_HEAD_DIM = 128

V7X_VMEM_BYTES = 64 * 1024 * 1024
VMEM_LIMIT_BYTES = V7X_VMEM_BYTES * 3 // 4
V7X_LANES = 128
V7X_MXU_WIDTH = 256
BF16_TILE_ROWS = 16
TOKEN_TILE = 512
WIDE_TOKEN_TILE = 1024
MASKED = -1e30
LOG2E = math.log2(math.e)

_NT = (((1,), (1,)), ((), ()))


def _params(*semantics, resident_bytes=0):
    limit = min(max(VMEM_LIMIT_BYTES, resident_bytes + V7X_VMEM_BYTES // 4), V7X_VMEM_BYTES * 7 // 8)
    return pltpu.CompilerParams(dimension_semantics=semantics, vmem_limit_bytes=limit)


def _resident(shape, layer=None):
    zeros = (0,) * len(shape)
    if layer is None:
        return pl.BlockSpec(shape, lambda *_: zeros, pipeline_mode=pl.Buffered(1))
    return pl.BlockSpec((None,) + tuple(shape), lambda *_: (layer,) + zeros,
                        pipeline_mode=pl.Buffered(1))


def _layer_bytes(w):
    return math.prod(w.shape[1:]) * w.dtype.itemsize


def _rms(x, g):
    return x * lax.rsqrt(jnp.mean(x * x, axis=-1, keepdims=True) + EPS) * g


def _dot(a, b):
    return jnp.dot(a, b, preferred_element_type=jnp.float32)


def _dot_nt(a, b):
    return lax.dot_general(a, b, _NT, preferred_element_type=jnp.float32)


def _half_swiglu(x, g_ref, wg_ref, wu_ref, wd_ref, ff_chunk):
    bf16 = jnp.bfloat16
    n = _rms(x, g_ref[...]).astype(bf16)
    d_ff = wg_ref.shape[1]
    acc = jnp.zeros(x.shape, jnp.float32)
    for c in range(d_ff // ff_chunk):
        cols = slice(c * ff_chunk, (c + 1) * ff_chunk)
        gate = _dot(n, wg_ref[:, cols].astype(bf16))
        up = _dot(n, wu_ref[:, cols].astype(bf16))
        h = (gate * jax.nn.sigmoid(gate) * up).astype(bf16)
        acc = acc + _dot(h, wd_ref[cols, :].astype(bf16))
    return x + 0.5 * acc


def _ffn_kernel(x_ref, g_ref, wg_ref, wu_ref, wd_ref, o_ref, *, ff_chunk):
    o_ref[...] = _half_swiglu(x_ref[...], g_ref, wg_ref, wu_ref, wd_ref, ff_chunk)


def _ffn(x, g, wg, wu, wd, layer, *, tm=TOKEN_TILE, ff_chunk=V7X_MXU_WIDTH):
    t, d = x.shape
    d_ff = wg.shape[2]
    assert t % tm == 0 and d_ff % ff_chunk == 0
    row = pl.BlockSpec((tm, d), lambda i: (i, 0))
    weight_bytes = sum(_layer_bytes(w) for w in (wg, wu, wd))
    return pl.pallas_call(
        functools.partial(_ffn_kernel, ff_chunk=ff_chunk),
        grid=(t // tm,),
        in_specs=[row, _resident((1, d), layer), _resident((d, d_ff), layer),
                  _resident((d, d_ff), layer), _resident((d_ff, d), layer)],
        out_specs=row,
        out_shape=jax.ShapeDtypeStruct((t, d), jnp.float32),
        compiler_params=_params("parallel", resident_bytes=weight_bytes),
        name="ffn",
    )(x, g, wg, wu, wd)


def _norm_proj_kernel(x_ref, g_ref, w_ref, o_ref):
    n = _rms(x_ref[...], g_ref[...]).astype(jnp.bfloat16)
    o_ref[...] = _dot(n, w_ref[...]).astype(o_ref.dtype)


def _norm_proj(x, g, w, *, tm=TOKEN_TILE):
    t, d = x.shape
    n_out = w.shape[1]
    assert t % tm == 0
    return pl.pallas_call(
        _norm_proj_kernel,
        grid=(t // tm,),
        in_specs=[pl.BlockSpec((tm, d), lambda i: (i, 0)), _resident((1, d)), _resident((d, n_out))],
        out_specs=pl.BlockSpec((tm, n_out), lambda i: (i, 0)),
        out_shape=jax.ShapeDtypeStruct((t, n_out), jnp.bfloat16),
        compiler_params=_params("parallel"),
        name="norm_proj",
    )(x, g, w)


def _mixer_proj_kernel(x_ref, g_ref, w_ref, wvt_ref, z_ref, vt_ref, *, head_dim):
    n = _rms(x_ref[...], g_ref[...]).astype(jnp.bfloat16)
    z_ref[...] = _dot(n, w_ref[...]).astype(z_ref.dtype)
    vt = _dot_nt(wvt_ref[...], n).astype(vt_ref.dtype)
    sum_rows = vt_ref.shape[2] - head_dim
    for h in range(vt_ref.shape[1]):
        vt_ref[0, h, 0:head_dim] = vt[h * head_dim:(h + 1) * head_dim, :]
        vt_ref[0, h, head_dim:] = jnp.ones((sum_rows, vt.shape[1]), vt_ref.dtype)


def _mixer_proj(x, g, w, w_vt, *, head_dim, sum_rows, seq, tm=WIDE_TOKEN_TILE):
    t, d = x.shape
    n_out = w.shape[1]
    kv_heads = w_vt.shape[0] // head_dim
    rows = head_dim + sum_rows
    assert seq % tm == 0
    tiles_per_seq = seq // tm
    return pl.pallas_call(
        functools.partial(_mixer_proj_kernel, head_dim=head_dim),
        grid=(t // tm,),
        in_specs=[pl.BlockSpec((tm, d), lambda i: (i, 0)), _resident((1, d)), _resident(w.shape),
                  _resident(w_vt.shape)],
        out_specs=[
            pl.BlockSpec((tm, n_out), lambda i: (i, 0)),
            pl.BlockSpec((1, kv_heads, rows, tm),
                         lambda i: (i // tiles_per_seq, 0, 0, i % tiles_per_seq)),
        ],
        out_shape=[
            jax.ShapeDtypeStruct((t, n_out), jnp.bfloat16),
            jax.ShapeDtypeStruct((t // seq, kv_heads, rows, seq), jnp.bfloat16),
        ],
        compiler_params=_params("parallel"),
        name="mixer_proj",
    )(x, g, w, w_vt)


def _rel_bucket(dist):
    n = np.maximum(dist, 0)
    max_exact = REL_BUCKETS // 2
    nf = np.maximum(n, 1).astype(np.float32)
    scaled = (np.log(nf / np.float32(max_exact)) / np.float32(math.log(REL_MAX_DIST / max_exact))
              * np.float32(REL_BUCKETS - max_exact))
    large = np.minimum(max_exact + scaled.astype(np.int32), REL_BUCKETS - 1)
    return np.where(n < max_exact, n, large).astype(np.int32)


def _bias_tile(table, n_keys, n_queries, offset, visible):
    period = n_keys + n_queries
    j = np.arange(period)
    dist = np.where(j < n_queries, j, j - period) + offset
    vec = jnp.where(visible(dist)[:, None], table[_rel_bucket(dist)] * LOG2E, MASKED).T
    spread = jnp.tile(vec, (1, n_keys))[:, :n_keys * (period - 1)]
    return spread.reshape(-1, n_keys, period - 1)[:, :, :n_queries]


def _even_mixer_kernel(sinks_ref, z_ref, kh_ref, vt_ref, vth_ref, ch_ref, uh_ref, convw_ref,
                       bias_ref, y_ref, *, halo_rows):
    i = pl.program_id(1)
    ts = z_ref.shape[1]
    c1, c2, c3 = CONV_CH, 2 * CONV_CH, 3 * CONV_CH
    q0 = c3
    k0 = q0 + SWA_HEADS * SWA_HEAD_DIM
    group = SWA_HEADS // SWA_KV_HEADS
    dh = SWA_HEAD_DIM
    first = i == 0

    f32 = jnp.float32
    v = z_ref[0, :, c1:c2].astype(f32) * z_ref[0, :, c2:c3].astype(f32)
    vh = ch_ref[0].astype(f32) * uh_ref[0].astype(f32)
    vh = jnp.where(first, 0.0, vh)
    row = lax.broadcasted_iota(jnp.int32, v.shape, 0)
    h1 = vh[halo_rows - 1:halo_rows, :]
    h2 = vh[halo_rows - 2:halo_rows - 1, :]
    v1 = jnp.where(row == 0, h1, pltpu.roll(v, 1, 0))
    v2 = jnp.where(row == 0, h2, jnp.where(row == 1, h1, pltpu.roll(v, 2, 0)))
    conv = v2 * convw_ref[0:1, :] + v1 * convw_ref[1:2, :] + v * convw_ref[2:3, :]
    y_ref[0, :, 0:CONV_CH] = (z_ref[0, :, 0:c1].astype(f32) * conv).astype(y_ref.dtype)

    c_s = dh ** -0.5 * LOG2E
    lanes = group * WINDOW
    lane = lax.broadcasted_iota(jnp.int32, (1, lanes), 1)
    for kh in range(SWA_KV_HEADS):
        sink = jnp.zeros((1, lanes), f32)
        for g in range(group):
            sink = jnp.where(lane // WINDOW == g, sinks_ref[kh * group + g] * LOG2E, sink)
        kcol = slice(k0 + kh * dh, k0 + (kh + 1) * dh)
        for qb in range(ts // WINDOW):
            rows = slice(qb * WINDOW, (qb + 1) * WINDOW)
            if qb == 0:
                kc = jnp.concatenate([kh_ref[0, :, kh * dh:(kh + 1) * dh], z_ref[0, rows, kcol]],
                                     axis=0)
                vt = jnp.concatenate([vth_ref[0, kh], vt_ref[0, kh, :, 0:WINDOW]], axis=1)
                bias = bias_ref[kh, jnp.where(first, 1, 0)]
            else:
                kc = z_ref[0, (qb - 1) * WINDOW:(qb + 1) * WINDOW, kcol]
                vt = vt_ref[0, kh, :, (qb - 1) * WINDOW:(qb + 1) * WINDOW]
                bias = bias_ref[kh, 0]
            qs = jnp.concatenate(
                [z_ref[0, rows, q0 + (kh * group + g) * dh:q0 + (kh * group + g + 1) * dh]
                 for g in range(group)], axis=0)
            e = _dot_nt(kc, qs) * c_s + bias
            m = jnp.maximum(jnp.max(e, axis=0, keepdims=True), sink)
            p = jnp.exp2(e - m).astype(jnp.bfloat16)
            acc = _dot(vt, p)
            denom = acc[dh:dh + 1] + jnp.exp2(sink - m)
            out = (acc / denom).T
            for g in range(group):
                head = kh * group + g
                y_ref[0, rows, CONV_CH + head * dh:CONV_CH + (head + 1) * dh] = (
                    out[g * WINDOW:(g + 1) * WINDOW, 0:dh].astype(y_ref.dtype))


def _even_mixer(z, vt, conv_w, sinks, table, *, ts=WIDE_TOKEN_TILE, halo_rows=BF16_TILE_ROWS):
    b, s, n_in = z.shape
    n_out = CONV_CH + SWA_HEADS * SWA_HEAD_DIM
    group = SWA_HEADS // SWA_KV_HEADS
    assert s % ts == 0 and ts % WINDOW == 0 and vt.shape[2] == V7X_LANES
    assert conv_w.shape == (CONV_WIDTH, CONV_CH) and CONV_WIDTH == 3 <= halo_rows
    banded = _bias_tile(table, 2 * WINDOW, WINDOW, WINDOW, lambda d: (d >= 0) & (d < WINDOW))
    own_only = jnp.where(np.arange(2 * WINDOW)[:, None] >= WINDOW, banded, MASKED)

    def by_kv_head(tile):
        tile = tile.reshape(SWA_KV_HEADS, group, 2 * WINDOW, WINDOW)
        return jnp.moveaxis(tile, 1, 2).reshape(SWA_KV_HEADS, 2 * WINDOW, group * WINDOW)

    bias = jnp.stack([by_kv_head(banded), by_kv_head(own_only)], axis=1)
    k_w = SWA_KV_HEADS * SWA_HEAD_DIM
    k_block = (3 * CONV_CH + SWA_HEADS * SWA_HEAD_DIM) // k_w
    w_per_tile = ts // WINDOW
    h_per_tile = ts // halo_rows
    prev_window = lambda i: jnp.maximum(i * w_per_tile - 1, 0)
    prev_rows = lambda i: jnp.maximum(i * h_per_tile - 1, 0)
    return pl.pallas_call(
        functools.partial(_even_mixer_kernel, halo_rows=halo_rows),
        grid=(b, s // ts),
        in_specs=[
            pl.BlockSpec(memory_space=pltpu.SMEM),
            pl.BlockSpec((1, ts, n_in), lambda bi, i: (bi, i, 0)),
            pl.BlockSpec((1, WINDOW, k_w), lambda bi, i: (bi, prev_window(i), k_block)),
            pl.BlockSpec((1,) + vt.shape[1:3] + (ts,), lambda bi, i: (bi, 0, 0, i)),
            pl.BlockSpec((1,) + vt.shape[1:3] + (WINDOW,), lambda bi, i: (bi, 0, 0, prev_window(i))),
            pl.BlockSpec((1, halo_rows, CONV_CH), lambda bi, i: (bi, prev_rows(i), 1)),
            pl.BlockSpec((1, halo_rows, CONV_CH), lambda bi, i: (bi, prev_rows(i), 2)),
            _resident(conv_w.shape),
            _resident(bias.shape),
        ],
        out_specs=pl.BlockSpec((1, ts, n_out), lambda bi, i: (bi, i, 0)),
        out_shape=jax.ShapeDtypeStruct((b, s, n_out), jnp.bfloat16),
        compiler_params=_params("parallel", "parallel"),
        name="even_mixer",
    )(sinks, z, z, vt, vt, z, z, conv_w, bias)


def _moba_kernel(q_ref, k_ref, vt_ref, bias_ref, o_ref,
                 kaug_ref, kmean_ref, neg_ref, qaug_ref, e_ref,
                 *, q_blocks, chunk_blocks):
    step = pl.program_id(2)
    blk = MOBA_BLOCK
    dh = MOBA_HEAD_DIM
    nblk = k_ref.shape[1] // blk
    group = q_ref.shape[2] // dh
    rows = q_blocks * group * blk
    f32, bf16 = jnp.float32, jnp.bfloat16
    c1 = dh ** -0.5 * LOG2E

    @pl.when(step == 0)
    def _():
        onehot_row = lax.broadcasted_iota(jnp.int32, (blk, dh), 1)
        means = []
        for j in range(nblk):
            kj = k_ref[0, j * blk:(j + 1) * blk, :]
            kaug_ref[j * blk:(j + 1) * blk, 0:dh] = kj
            kaug_ref[j * blk:(j + 1) * blk, dh:2 * dh] = (onehot_row == j).astype(bf16)
            means.append(jnp.mean(kj.astype(f32), axis=0, keepdims=True))
        rest = jnp.concatenate(means, axis=0)
        for part in range(3):
            term = rest.astype(bf16)
            kmean_ref[part * nblk:(part + 1) * nblk, :] = term
            rest = rest - term.astype(f32)
        neg_ref[...] = jnp.zeros(neg_ref.shape, f32)

    q = q_ref[0]
    qs = jnp.concatenate([q[sb * blk:(sb + 1) * blk, g * dh:(g + 1) * dh]
                          for sb in range(q_blocks) for g in range(group)], axis=0)
    qaug_ref[:, 0:dh] = qs
    lane = lax.broadcasted_iota(jnp.int32, (1, rows), 1)
    own = step * q_blocks + lane // (group * blk)

    gate3 = _dot_nt(kmean_ref[...], qs)
    gate = gate3[0:nblk] + gate3[nblk:2 * nblk] + gate3[2 * nblk:3 * nblk]
    blkid = lax.broadcasted_iota(jnp.int32, gate.shape, 0)
    remaining = jnp.where(blkid < own, gate, -jnp.inf)
    keep = blkid >= own
    for _ in range(MOBA_TOPK):
        best = jnp.max(remaining, axis=0, keepdims=True)
        first = jnp.min(jnp.where(remaining == best, blkid, nblk), axis=0, keepdims=True)
        taken = (blkid == first) & (best > -jnp.inf)
        keep = keep | taken
        remaining = jnp.where(taken, -jnp.inf, remaining)
    neg_ref[0:nblk, :] = jnp.where(keep, 0.0, MASKED)
    qaug_ref[:, dh:2 * dh] = neg_ref[...].T.astype(bf16)

    def attend(first_q_block):
        width = group * blk

        def chunks(sb):
            n = first_q_block + sb + 1
            return [(c0, min(chunk_blocks, n - c0)) for c0 in range(0, n, chunk_blocks)]

        def exponents(sb, c0, cb, m):
            lanes = slice(sb * width, (sb + 1) * width)
            s = _dot_nt(kaug_ref[c0 * blk:(c0 + cb) * blk, :], qaug_ref[lanes, :])
            for r in range(cb):
                j = c0 + r
                tile = min(first_q_block + sb - j, 2)
                e = s[r * blk:(r + 1) * blk] * c1 + bias_ref[0, tile]
                e_ref[sb % 2, j * blk:(j + 1) * blk, :] = e
                e_max = jnp.max(e, axis=0, keepdims=True)
                m = e_max if m is None else jnp.maximum(m, e_max)
            return m

        def weighted_values(sb, j, m, acc):
            p = jnp.exp2(e_ref[sb % 2, j * blk:(j + 1) * blk, :] - m).astype(bf16)
            return acc + _dot(vt_ref[0, 0, :, j * blk:(j + 1) * blk], p)

        m = [None] * q_blocks
        for sb in range(q_blocks + 1):
            pass1 = chunks(sb) if sb < q_blocks else []
            pass2 = chunks(sb - 1) if sb > 0 else []
            acc = jnp.zeros((vt_ref.shape[2], width), f32)
            for ci in range(max(len(pass1), len(pass2))):
                if ci < len(pass1):
                    m[sb] = exponents(sb, *pass1[ci], m[sb])
                if ci < len(pass2):
                    c0, cb = pass2[ci]
                    for j in range(c0, c0 + cb):
                        acc = weighted_values(sb - 1, j, m[sb - 1], acc)
            if sb > 0:
                out = (acc[0:dh] / acc[dh:dh + 1]).T.astype(o_ref.dtype)
                rows_out = slice((sb - 1) * blk, sb * blk)
                for g in range(group):
                    o_ref[0, rows_out, g * dh:(g + 1) * dh] = out[g * blk:(g + 1) * blk]

    for case in range(nblk // q_blocks):
        @pl.when(step == case)
        def _(case=case):
            attend(case * q_blocks)


def _moba(qk, vt, table, *, q_blocks=8, chunk_blocks=8):
    b, s, _ = qk.shape
    blk, dh = MOBA_BLOCK, MOBA_HEAD_DIM
    assert s % blk == 0 and dh == V7X_LANES
    nblk = s // blk
    q_blocks = q_blocks or nblk
    assert nblk % q_blocks == 0 and nblk <= V7X_LANES
    group = MOBA_HEADS // MOBA_KV_HEADS
    rows = group * blk

    def tile(blocks_back):
        t = _bias_tile(table, blk, blk, blocks_back * blk, lambda d: d >= 0)
        t = t.reshape(MOBA_KV_HEADS, group, blk, blk)
        return jnp.moveaxis(t, 1, 2).reshape(MOBA_KV_HEADS, blk, rows)

    bias = jnp.stack([tile(0), tile(1), tile(2)], axis=1)
    assert (_rel_bucket(np.arange(blk + 1, 2 * blk * nblk)) == REL_BUCKETS - 1).all()
    k_block0 = MOBA_HEADS
    return pl.pallas_call(
        functools.partial(_moba_kernel, q_blocks=q_blocks, chunk_blocks=chunk_blocks),
        grid=(b, MOBA_KV_HEADS, nblk // q_blocks),
        in_specs=[
            pl.BlockSpec((1, q_blocks * blk, group * dh), lambda bi, h, i: (bi, i, h)),
            pl.BlockSpec((1, s, dh), lambda bi, h, i: (bi, 0, k_block0 + h)),
            pl.BlockSpec((1, 1, vt.shape[2], s), lambda bi, h, i: (bi, h, 0, 0)),
            pl.BlockSpec((1, 3, blk, rows), lambda bi, h, i: (h, 0, 0, 0)),
        ],
        out_specs=pl.BlockSpec((1, q_blocks * blk, group * dh), lambda bi, h, i: (bi, i, h)),
        out_shape=jax.ShapeDtypeStruct((b, s, MOBA_HEADS * dh), jnp.bfloat16),
        scratch_shapes=[
            pltpu.VMEM((s, 2 * dh), jnp.bfloat16),
            pltpu.VMEM((3 * nblk, dh), jnp.bfloat16),
            pltpu.VMEM((V7X_LANES, q_blocks * rows), jnp.float32),
            pltpu.VMEM((q_blocks * rows, 2 * dh), jnp.bfloat16),
            pltpu.VMEM((2, s, rows), jnp.float32),
        ],
        compiler_params=_params("parallel", "parallel", "arbitrary"),
        name="moba",
    )(qk, qk, vt, bias)


def _cross_attention(x, g_ref, wq_ref, kv_ref, wo_ref, a_ref):
    dh = XA_HEAD_DIM
    n = _rms(x, g_ref[...]).astype(jnp.bfloat16)
    q = _dot(n, wq_ref[...]).astype(jnp.bfloat16)
    scale = dh ** -0.5
    v0 = XA_HEADS * dh
    for h in range(XA_HEADS):
        cols = slice(h * dh, (h + 1) * dh)
        s = _dot_nt(q[:, cols], kv_ref[:, cols]) * scale
        m = jnp.max(s, axis=-1, keepdims=True)
        p = jnp.exp(s - m)
        l = jnp.sum(p, axis=-1, keepdims=True)
        out = _dot(p.astype(jnp.bfloat16), kv_ref[:, v0 + h * dh:v0 + (h + 1) * dh]) / l
        a_ref[:, cols] = out.astype(a_ref.dtype)
    return x + _dot(a_ref[...], wo_ref[...])


def _post_mixer_kernel(x_ref, y_ref, wmix_ref, xg_ref, wq_ref, kv_ref, wo_ref, fg_ref, wg_ref,
                       wu_ref, wd_ref, ng_ref, o_ref, a_ref, *, ff_chunk, final_norm):
    x = x_ref[...] + _dot(y_ref[...], wmix_ref[...])
    x = _cross_attention(x, xg_ref, wq_ref, kv_ref, wo_ref, a_ref)
    x = _half_swiglu(x, fg_ref, wg_ref, wu_ref, wd_ref, ff_chunk)
    if final_norm:
        x = _rms(x, ng_ref[...])
    o_ref[...] = x


def _post_mixer(x, y, w_mix, xa_g, w_q, kv, layer, w_o, ffn_g, wg, wu, wd, norm_g, *, final_norm,
                seq, mem_len, tm=TOKEN_TILE, ff_chunk=V7X_MXU_WIDTH):
    t, d = x.shape
    d_ff = wg.shape[2]
    xa_w = XA_HEADS * XA_HEAD_DIM
    assert seq % tm == 0 and d_ff % ff_chunk == 0
    tiles_per_seq = seq // tm
    row = pl.BlockSpec((tm, d), lambda i: (i, 0))
    weight_bytes = (math.prod(w_mix.shape) * w_mix.dtype.itemsize
                    + sum(_layer_bytes(w) for w in (w_q, w_o, wg, wu, wd)))
    return pl.pallas_call(
        functools.partial(_post_mixer_kernel, ff_chunk=ff_chunk, final_norm=final_norm),
        grid=(t // tm,),
        in_specs=[
            row,
            pl.BlockSpec((tm, y.shape[1]), lambda i: (i, 0)),
            _resident(w_mix.shape),
            _resident((1, d), layer),
            _resident((d, xa_w), layer),
            pl.BlockSpec((mem_len, 2 * xa_w), lambda i: (i // tiles_per_seq, layer)),
            _resident((xa_w, d), layer),
            _resident((1, d), layer),
            _resident((d, d_ff), layer),
            _resident((d, d_ff), layer),
            _resident((d_ff, d), layer),
            _resident((1, d)),
        ],
        out_specs=row,
        out_shape=jax.ShapeDtypeStruct((t, d), jnp.float32),
        scratch_shapes=[pltpu.VMEM((tm, xa_w), jnp.bfloat16)],
        compiler_params=_params("parallel", resident_bytes=weight_bytes),
        name="post_mixer",
    )(x, y, w_mix, xa_g, w_q, kv, w_o, ffn_g, wg, wu, wd, norm_g)


def kernel(x, mem, ffn1_norm, ffn1_w_gate, ffn1_w_up, ffn1_w_down, mix_norm, ev_w_in, ev_conv_w,
           ev_sinks, ev_w_out, od_w_in, od_w_out, rel_bias, xa_norm, xa_w_q, xa_w_kv, xa_w_o,
           mem_norm, ffn2_norm, ffn2_w_gate, ffn2_w_up, ffn2_w_down, final_norm):
    b, s, d = x.shape
    depth = ffn1_norm.shape[0]
    mem_len = mem.shape[1]
    bf16 = jnp.bfloat16
    row = lambda g: g.reshape(1, d)

    kv_w = jnp.concatenate([xa_w_kv[l] for l in range(depth)], axis=1).astype(bf16)
    kv = _norm_proj(mem.reshape(b * mem_len, d), row(mem_norm), kv_w,
                    tm=min(TOKEN_TILE, b * mem_len))

    gains = lambda g: g.reshape(depth, 1, d)
    ffn1 = (gains(ffn1_norm), ffn1_w_gate, ffn1_w_up, ffn1_w_down)
    ffn2 = (gains(ffn2_norm), ffn2_w_gate, ffn2_w_up, ffn2_w_down)
    xa_q, xa_o = xa_w_q.astype(bf16), xa_w_o.astype(bf16)

    xt = x.reshape(b * s, d)
    for l in range(depth):
        xt = _ffn(xt, *ffn1, l)
        i = l // 2
        if l % 2 == 0:
            v0 = 3 * CONV_CH + (SWA_HEADS + SWA_KV_HEADS) * SWA_HEAD_DIM
            w_in = ev_w_in[i].astype(bf16)
            z, vt = _mixer_proj(xt, row(mix_norm[l]), w_in[:, :v0], w_in[:, v0:].T,
                                head_dim=SWA_HEAD_DIM, sum_rows=SWA_SUM_ROWS, seq=s)
            y = _even_mixer(z.reshape(b, s, -1), vt, ev_conv_w[i], ev_sinks[i], rel_bias)
            w_mix = ev_w_out[i]
        else:
            v0 = (MOBA_HEADS + MOBA_KV_HEADS) * MOBA_HEAD_DIM
            w_in = od_w_in[i].astype(bf16)
            qk, vt = _mixer_proj(xt, row(mix_norm[l]), w_in[:, :v0], w_in[:, v0:].T,
                                 head_dim=MOBA_HEAD_DIM, sum_rows=MOBA_SUM_ROWS, seq=s)
            y = _moba(qk.reshape(b, s, -1), vt, rel_bias)
            w_mix = od_w_out[i]
        xt = _post_mixer(xt, y.reshape(b * s, -1), w_mix.astype(bf16), gains(xa_norm), xa_q, kv, l,
                         xa_o, *ffn2, row(final_norm), final_norm=(l == depth - 1), seq=s,
                         mem_len=mem_len)
    return xt.reshape(b, s, d)
```

```python
import functools
import math

import jax
import jax.numpy as jnp
import numpy as np
from jax import lax
from jax.experimental import pallas as pl
from jax.experimental.pallas import tpu as pltpu

EPS = 1e-6
CONV_WIDTH = 3
CONV_CH = 512
SWA_HEADS = 8
SWA_KV_HEADS = 2
SWA_HEAD_DIM = 64
WINDOW = 128
MOBA_HEADS = 8
MOBA_KV_HEADS = 4
MOBA_HEAD_DIM = 128
MOBA_BLOCK = 256
MOBA_TOPK = 3
MOBA_SUM_ROWS = 16
SWA_SUM_ROWS = 64
REL_BUCKETS = 32
REL_MAX_DIST = 128
XA_HEADS = 4
XA_HEAD_DIM = 128

V7X_VMEM_BYTES = 64 * 1024 * 1024
VMEM_LIMIT_BYTES = V7X_VMEM_BYTES * 3 // 4
V7X_LANES = 128
V7X_MXU_WIDTH = 256
BF16_TILE_ROWS = 16
TOKEN_TILE = 512
WIDE_TOKEN_TILE = 1024
MASKED = -1e30
LOG2E = math.log2(math.e)

_NT = (((1,), (1,)), ((), ()))


def _params(*semantics, resident_bytes=0):
    limit = min(max(VMEM_LIMIT_BYTES, resident_bytes + V7X_VMEM_BYTES // 4), V7X_VMEM_BYTES * 7 // 8)
    return pltpu.CompilerParams(dimension_semantics=semantics, vmem_limit_bytes=limit)


def _resident(shape, layer=None):
    zeros = (0,) * len(shape)
    if layer is None:
        return pl.BlockSpec(shape, lambda *_: zeros, pipeline_mode=pl.Buffered(1))
    return pl.BlockSpec((None,) + tuple(shape), lambda *_: (layer,) + zeros,
                        pipeline_mode=pl.Buffered(1))


def _layer_bytes(w):
    return math.prod(w.shape[1:]) * w.dtype.itemsize


def _rms(x, g):
    return x * lax.rsqrt(jnp.mean(x * x, axis=-1, keepdims=True) + EPS) * g


def _dot(a, b):
    return jnp.dot(a, b, preferred_element_type=jnp.float32)


def _dot_nt(a, b):
    return lax.dot_general(a, b, _NT, preferred_element_type=jnp.float32)


def _half_swiglu(x, g_ref, wg_ref, wu_ref, wd_ref, ff_chunk):
    bf16 = jnp.bfloat16
    n = _rms(x, g_ref[...]).astype(bf16)
    d_ff = wg_ref.shape[1]
    acc = jnp.zeros(x.shape, jnp.float32)
    for c in range(d_ff // ff_chunk):
        cols = slice(c * ff_chunk, (c + 1) * ff_chunk)
        gate = _dot(n, wg_ref[:, cols].astype(bf16))
        up = _dot(n, wu_ref[:, cols].astype(bf16))
        h = (gate * jax.nn.sigmoid(gate) * up).astype(bf16)
        acc = acc + _dot(h, wd_ref[cols, :].astype(bf16))
    return x + 0.5 * acc


def _ffn_kernel(x_ref, g_ref, wg_ref, wu_ref, wd_ref, o_ref, *, ff_chunk):
    o_ref[...] = _half_swiglu(x_ref[...], g_ref, wg_ref, wu_ref, wd_ref, ff_chunk)


def _ffn(x, g, wg, wu, wd, layer, *, tm=TOKEN_TILE, ff_chunk=V7X_MXU_WIDTH):
    t, d = x.shape
    d_ff = wg.shape[2]
    assert t % tm == 0 and d_ff % ff_chunk == 0
    row = pl.BlockSpec((tm, d), lambda i: (i, 0))
    weight_bytes = sum(_layer_bytes(w) for w in (wg, wu, wd))
    return pl.pallas_call(
        functools.partial(_ffn_kernel, ff_chunk=ff_chunk),
        grid=(t // tm,),
        in_specs=[row, _resident((1, d), layer), _resident((d, d_ff), layer),
                  _resident((d, d_ff), layer), _resident((d_ff, d), layer)],
        out_specs=row,
        out_shape=jax.ShapeDtypeStruct((t, d), jnp.float32),
        compiler_params=_params("parallel", resident_bytes=weight_bytes),
        name="ffn",
    )(x, g, wg, wu, wd)


def _norm_proj_kernel(x_ref, g_ref, w_ref, o_ref):
    n = _rms(x_ref[...], g_ref[...]).astype(jnp.bfloat16)
    o_ref[...] = _dot(n, w_ref[...]).astype(o_ref.dtype)


def _norm_proj(x, g, w, *, tm=TOKEN_TILE):
    t, d = x.shape
    n_out = w.shape[1]
    assert t % tm == 0
    return pl.pallas_call(
        _norm_proj_kernel,
        grid=(t // tm,),
        in_specs=[pl.BlockSpec((tm, d), lambda i: (i, 0)), _resident((1, d)), _resident((d, n_out))],
        out_specs=pl.BlockSpec((tm, n_out), lambda i: (i, 0)),
        out_shape=jax.ShapeDtypeStruct((t, n_out), jnp.bfloat16),
        compiler_params=_params("parallel"),
        name="norm_proj",
    )(x, g, w)


def _mixer_proj_kernel(x_ref, g_ref, w_ref, z_ref, vt_ref, *, head_dim):
    n = _rms(x_ref[...], g_ref[...]).astype(jnp.bfloat16)
    zv = _dot(n, w_ref[...])
    n_z = z_ref.shape[1]
    z_ref[...] = zv[:, 0:n_z].astype(z_ref.dtype)
    vt = zv[:, n_z:].T.astype(vt_ref.dtype)
    sum_rows = vt_ref.shape[2] - head_dim
    for h in range(vt_ref.shape[1]):
        vt_ref[0, h, 0:head_dim] = vt[h * head_dim:(h + 1) * head_dim, :]
        vt_ref[0, h, head_dim:] = jnp.ones((sum_rows, vt.shape[1]), vt_ref.dtype)


def _mixer_proj(x, g, w, *, kv_heads, head_dim, sum_rows, seq, tm=WIDE_TOKEN_TILE):
    t, d = x.shape
    n_out = w.shape[1] - kv_heads * head_dim
    rows = head_dim + sum_rows
    assert seq % tm == 0 and n_out % V7X_LANES == 0
    tiles_per_seq = seq // tm
    return pl.pallas_call(
        functools.partial(_mixer_proj_kernel, head_dim=head_dim),
        grid=(t // tm,),
        in_specs=[pl.BlockSpec((tm, d), lambda i: (i, 0)), _resident((1, d)), _resident(w.shape)],
        out_specs=[
            pl.BlockSpec((tm, n_out), lambda i: (i, 0)),
            pl.BlockSpec((1, kv_heads, rows, tm),
                         lambda i: (i // tiles_per_seq, 0, 0, i % tiles_per_seq)),
        ],
        out_shape=[
            jax.ShapeDtypeStruct((t, n_out), jnp.bfloat16),
            jax.ShapeDtypeStruct((t // seq, kv_heads, rows, seq), jnp.bfloat16),
        ],
        compiler_params=_params("parallel"),
        name="mixer_proj",
    )(x, g, w)


def _rel_bucket(dist):
    n = np.maximum(dist, 0)
    max_exact = REL_BUCKETS // 2
    nf = np.maximum(n, 1).astype(np.float32)
    scaled = (np.log(nf / np.float32(max_exact)) / np.float32(math.log(REL_MAX_DIST / max_exact))
              * np.float32(REL_BUCKETS - max_exact))
    large = np.minimum(max_exact + scaled.astype(np.int32), REL_BUCKETS - 1)
    return np.where(n < max_exact, n, large).astype(np.int32)


def _bias_tile(table, n_keys, n_queries, offset, visible):
    period = n_keys + n_queries
    j = np.arange(period)
    dist = np.where(j < n_queries, j, j - period) + offset
    vec = jnp.where(visible(dist)[:, None], table[_rel_bucket(dist)] * LOG2E, MASKED).T
    spread = jnp.tile(vec, (1, n_keys))[:, :n_keys * (period - 1)]
    return spread.reshape(-1, n_keys, period - 1)[:, :, :n_queries]


def _even_mixer_kernel(sinks_ref, z_ref, kh_ref, vt_ref, vth_ref, ch_ref, uh_ref, convw_ref,
                       bias_ref, y_ref, *, halo_rows):
    i = pl.program_id(1)
    ts = z_ref.shape[1]
    c1, c2, c3 = CONV_CH, 2 * CONV_CH, 3 * CONV_CH
    q0 = c3
    k0 = q0 + SWA_HEADS * SWA_HEAD_DIM
    group = SWA_HEADS // SWA_KV_HEADS
    dh = SWA_HEAD_DIM
    first = i == 0

    f32 = jnp.float32
    v = z_ref[0, :, c1:c2].astype(f32) * z_ref[0, :, c2:c3].astype(f32)
    vh = ch_ref[0].astype(f32) * uh_ref[0].astype(f32)
    vh = jnp.where(first, 0.0, vh)
    row = lax.broadcasted_iota(jnp.int32, v.shape, 0)
    h1 = vh[halo_rows - 1:halo_rows, :]
    h2 = vh[halo_rows - 2:halo_rows - 1, :]
    v1 = jnp.where(row == 0, h1, pltpu.roll(v, 1, 0))
    v2 = jnp.where(row == 0, h2, jnp.where(row == 1, h1, pltpu.roll(v, 2, 0)))
    conv = v2 * convw_ref[0:1, :] + v1 * convw_ref[1:2, :] + v * convw_ref[2:3, :]
    y_ref[0, :, 0:CONV_CH] = (z_ref[0, :, 0:c1].astype(f32) * conv).astype(y_ref.dtype)

    c_s = dh ** -0.5 * LOG2E
    lanes = group * WINDOW
    lane = lax.broadcasted_iota(jnp.int32, (1, lanes), 1)
    for kh in range(SWA_KV_HEADS):
        sink = jnp.zeros((1, lanes), f32)
        for g in range(group):
            sink = jnp.where(lane // WINDOW == g, sinks_ref[kh * group + g] * LOG2E, sink)
        kcol = slice(k0 + kh * dh, k0 + (kh + 1) * dh)
        for qb in range(ts // WINDOW):
            rows = slice(qb * WINDOW, (qb + 1) * WINDOW)
            if qb == 0:
                kc = jnp.concatenate([kh_ref[0, :, kh * dh:(kh + 1) * dh], z_ref[0, rows, kcol]],
                                     axis=0)
                vt = jnp.concatenate([vth_ref[0, kh], vt_ref[0, kh, :, 0:WINDOW]], axis=1)
                bias = bias_ref[kh, jnp.where(first, 1, 0)]
            else:
                kc = z_ref[0, (qb - 1) * WINDOW:(qb + 1) * WINDOW, kcol]
                vt = vt_ref[0, kh, :, (qb - 1) * WINDOW:(qb + 1) * WINDOW]
                bias = bias_ref[kh, 0]
            qs = jnp.concatenate(
                [z_ref[0, rows, q0 + (kh * group + g) * dh:q0 + (kh * group + g + 1) * dh]
                 for g in range(group)], axis=0)
            e = _dot_nt(kc, qs) * c_s + bias
            m = jnp.maximum(jnp.max(e, axis=0, keepdims=True), sink)
            p = jnp.exp2(e - m).astype(jnp.bfloat16)
            acc = _dot(vt, p)
            denom = acc[dh:dh + 1] + jnp.exp2(sink - m)
            out = (acc / denom).T
            for g in range(group):
                head = kh * group + g
                y_ref[0, rows, CONV_CH + head * dh:CONV_CH + (head + 1) * dh] = (
                    out[g * WINDOW:(g + 1) * WINDOW, 0:dh].astype(y_ref.dtype))


def _even_mixer(z, vt, conv_w, sinks, table, *, ts=WIDE_TOKEN_TILE, halo_rows=BF16_TILE_ROWS):
    b, s, n_in = z.shape
    n_out = CONV_CH + SWA_HEADS * SWA_HEAD_DIM
    group = SWA_HEADS // SWA_KV_HEADS
    assert s % ts == 0 and ts % WINDOW == 0 and vt.shape[2] == V7X_LANES
    assert conv_w.shape == (CONV_WIDTH, CONV_CH) and CONV_WIDTH == 3 <= halo_rows
    banded = _bias_tile(table, 2 * WINDOW, WINDOW, WINDOW, lambda d: (d >= 0) & (d < WINDOW))
    own_only = jnp.where(np.arange(2 * WINDOW)[:, None] >= WINDOW, banded, MASKED)

    def by_kv_head(tile):
        tile = tile.reshape(SWA_KV_HEADS, group, 2 * WINDOW, WINDOW)
        return jnp.moveaxis(tile, 1, 2).reshape(SWA_KV_HEADS, 2 * WINDOW, group * WINDOW)

    bias = jnp.stack([by_kv_head(banded), by_kv_head(own_only)], axis=1)
    k_w = SWA_KV_HEADS * SWA_HEAD_DIM
    k_block = (3 * CONV_CH + SWA_HEADS * SWA_HEAD_DIM) // k_w
    w_per_tile = ts // WINDOW
    h_per_tile = ts // halo_rows
    prev_window = lambda i: jnp.maximum(i * w_per_tile - 1, 0)
    prev_rows = lambda i: jnp.maximum(i * h_per_tile - 1, 0)
    return pl.pallas_call(
        functools.partial(_even_mixer_kernel, halo_rows=halo_rows),
        grid=(b, s // ts),
        in_specs=[
            pl.BlockSpec(memory_space=pltpu.SMEM),
            pl.BlockSpec((1, ts, n_in), lambda bi, i: (bi, i, 0)),
            pl.BlockSpec((1, WINDOW, k_w), lambda bi, i: (bi, prev_window(i), k_block)),
            pl.BlockSpec((1,) + vt.shape[1:3] + (ts,), lambda bi, i: (bi, 0, 0, i)),
            pl.BlockSpec((1,) + vt.shape[1:3] + (WINDOW,), lambda bi, i: (bi, 0, 0, prev_window(i))),
            pl.BlockSpec((1, halo_rows, CONV_CH), lambda bi, i: (bi, prev_rows(i), 1)),
            pl.BlockSpec((1, halo_rows, CONV_CH), lambda bi, i: (bi, prev_rows(i), 2)),
            _resident(conv_w.shape),
            _resident(bias.shape),
        ],
        out_specs=pl.BlockSpec((1, ts, n_out), lambda bi, i: (bi, i, 0)),
        out_shape=jax.ShapeDtypeStruct((b, s, n_out), jnp.bfloat16),
        compiler_params=_params("parallel", "parallel"),
        name="even_mixer",
    )(sinks, z, z, vt, vt, z, z, conv_w, bias)


def _moba_kernel(q_ref, k_ref, vt_ref, bias_ref, o_ref,
                 kaug_ref, kmean_ref, neg_ref, qaug_ref, e_ref,
                 *, q_blocks, chunk_blocks):
    step = pl.program_id(2)
    blk = MOBA_BLOCK
    dh = MOBA_HEAD_DIM
    nblk = k_ref.shape[1] // blk
    group = q_ref.shape[2] // dh
    rows = q_blocks * group * blk
    f32, bf16 = jnp.float32, jnp.bfloat16
    c1 = dh ** -0.5 * LOG2E

    @pl.when(step == 0)
    def _():
        onehot_row = lax.broadcasted_iota(jnp.int32, (blk, dh), 1)
        means = []
        for j in range(nblk):
            kj = k_ref[0, j * blk:(j + 1) * blk, :]
            kaug_ref[j * blk:(j + 1) * blk, 0:dh] = kj
            kaug_ref[j * blk:(j + 1) * blk, dh:2 * dh] = (onehot_row == j).astype(bf16)
            means.append(jnp.mean(kj.astype(f32), axis=0, keepdims=True))
        rest = jnp.concatenate(means, axis=0)
        for part in range(3):
            term = rest.astype(bf16)
            kmean_ref[part * nblk:(part + 1) * nblk, :] = term
            rest = rest - term.astype(f32)
        neg_ref[...] = jnp.zeros(neg_ref.shape, f32)

    q = q_ref[0]
    qs = jnp.concatenate([q[sb * blk:(sb + 1) * blk, g * dh:(g + 1) * dh]
                          for sb in range(q_blocks) for g in range(group)], axis=0)
    qaug_ref[:, 0:dh] = qs
    lane = lax.broadcasted_iota(jnp.int32, (1, rows), 1)
    own = step * q_blocks + lane // (group * blk)

    gate3 = _dot_nt(kmean_ref[...], qs)
    gate = gate3[0:nblk] + gate3[nblk:2 * nblk] + gate3[2 * nblk:3 * nblk]
    blkid = lax.broadcasted_iota(jnp.int32, gate.shape, 0)
    remaining = jnp.where(blkid < own, gate, -jnp.inf)
    keep = blkid >= own
    for _ in range(MOBA_TOPK):
        best = jnp.max(remaining, axis=0, keepdims=True)
        first = jnp.min(jnp.where(remaining == best, blkid, nblk), axis=0, keepdims=True)
        taken = (blkid == first) & (best > -jnp.inf)
        keep = keep | taken
        remaining = jnp.where(taken, -jnp.inf, remaining)
    neg_ref[0:nblk, :] = jnp.where(keep, 0.0, MASKED)
    qaug_ref[:, dh:2 * dh] = neg_ref[...].T.astype(bf16)

    def attend(first_q_block):
        width = group * blk

        def chunks(sb):
            n = first_q_block + sb + 1
            return [(c0, min(chunk_blocks, n - c0)) for c0 in range(0, n, chunk_blocks)]

        def exponents(sb, c0, cb, m):
            lanes = slice(sb * width, (sb + 1) * width)
            s = _dot_nt(kaug_ref[c0 * blk:(c0 + cb) * blk, :], qaug_ref[lanes, :])
            for r in range(cb):
                j = c0 + r
                tile = min(first_q_block + sb - j, 2)
                e = s[r * blk:(r + 1) * blk] * c1 + bias_ref[0, tile]
                e_ref[sb % 2, j * blk:(j + 1) * blk, :] = e
                e_max = jnp.max(e, axis=0, keepdims=True)
                m = e_max if m is None else jnp.maximum(m, e_max)
            return m

        def weighted_values(sb, j, m, acc):
            p = jnp.exp2(e_ref[sb % 2, j * blk:(j + 1) * blk, :] - m).astype(bf16)
            return acc + _dot(vt_ref[0, 0, :, j * blk:(j + 1) * blk], p)

        m = [None] * q_blocks
        for sb in range(q_blocks + 1):
            pass1 = chunks(sb) if sb < q_blocks else []
            pass2 = chunks(sb - 1) if sb > 0 else []
            acc = jnp.zeros((vt_ref.shape[2], width), f32)
            for ci in range(max(len(pass1), len(pass2))):
                if ci < len(pass1):
                    m[sb] = exponents(sb, *pass1[ci], m[sb])
                if ci < len(pass2):
                    c0, cb = pass2[ci]
                    for j in range(c0, c0 + cb):
                        acc = weighted_values(sb - 1, j, m[sb - 1], acc)
            if sb > 0:
                out = (acc[0:dh] / acc[dh:dh + 1]).T.astype(o_ref.dtype)
                rows_out = slice((sb - 1) * blk, sb * blk)
                for g in range(group):
                    o_ref[0, rows_out, g * dh:(g + 1) * dh] = out[g * blk:(g + 1) * blk]

    for case in range(nblk // q_blocks):
        @pl.when(step == case)
        def _(case=case):
            attend(case * q_blocks)


def _moba(qk, vt, table, *, q_blocks=8, chunk_blocks=8):
    b, s, _ = qk.shape
    blk, dh = MOBA_BLOCK, MOBA_HEAD_DIM
    assert s % blk == 0 and dh == V7X_LANES
    nblk = s // blk
    q_blocks = q_blocks or nblk
    assert nblk % q_blocks == 0 and nblk <= V7X_LANES
    group = MOBA_HEADS // MOBA_KV_HEADS
    rows = group * blk

    def tile(blocks_back):
        t = _bias_tile(table, blk, blk, blocks_back * blk, lambda d: d >= 0)
        t = t.reshape(MOBA_KV_HEADS, group, blk, blk)
        return jnp.moveaxis(t, 1, 2).reshape(MOBA_KV_HEADS, blk, rows)

    bias = jnp.stack([tile(0), tile(1), tile(2)], axis=1)
    assert (_rel_bucket(np.arange(blk + 1, 2 * blk * nblk)) == REL_BUCKETS - 1).all()
    k_block0 = MOBA_HEADS
    return pl.pallas_call(
        functools.partial(_moba_kernel, q_blocks=q_blocks, chunk_blocks=chunk_blocks),
        grid=(b, MOBA_KV_HEADS, nblk // q_blocks),
        in_specs=[
            pl.BlockSpec((1, q_blocks * blk, group * dh), lambda bi, h, i: (bi, i, h)),
            pl.BlockSpec((1, s, dh), lambda bi, h, i: (bi, 0, k_block0 + h)),
            pl.BlockSpec((1, 1, vt.shape[2], s), lambda bi, h, i: (bi, h, 0, 0)),
            pl.BlockSpec((1, 3, blk, rows), lambda bi, h, i: (h, 0, 0, 0)),
        ],
        out_specs=pl.BlockSpec((1, q_blocks * blk, group * dh), lambda bi, h, i: (bi, i, h)),
        out_shape=jax.ShapeDtypeStruct((b, s, MOBA_HEADS * dh), jnp.bfloat16),
        scratch_shapes=[
            pltpu.VMEM((s, 2 * dh), jnp.bfloat16),
            pltpu.VMEM((3 * nblk, dh), jnp.bfloat16),
            pltpu.VMEM((V7X_LANES, q_blocks * rows), jnp.float32),
            pltpu.VMEM((q_blocks * rows, 2 * dh), jnp.bfloat16),
            pltpu.VMEM((2, s, rows), jnp.float32),
        ],
        compiler_params=_params("parallel", "parallel", "arbitrary"),
        name="moba",
    )(qk, qk, vt, bias)


def _cross_attention(x, g_ref, wq_ref, kv_ref, wo_ref, a_ref):
    dh = XA_HEAD_DIM
    n = _rms(x, g_ref[...]).astype(jnp.bfloat16)
    q = _dot(n, wq_ref[...]).astype(jnp.bfloat16)
    scale = dh ** -0.5
    v0 = XA_HEADS * dh
    for h in range(XA_HEADS):
        cols = slice(h * dh, (h + 1) * dh)
        s = _dot_nt(q[:, cols], kv_ref[:, cols]) * scale
        m = jnp.max(s, axis=-1, keepdims=True)
        p = jnp.exp(s - m)
        l = jnp.sum(p, axis=-1, keepdims=True)
        out = _dot(p.astype(jnp.bfloat16), kv_ref[:, v0 + h * dh:v0 + (h + 1) * dh]) / l
        a_ref[:, cols] = out.astype(a_ref.dtype)
    return x + _dot(a_ref[...], wo_ref[...])


def _post_mixer_kernel(x_ref, y_ref, wmix_ref, xg_ref, wq_ref, kv_ref, wo_ref, fg_ref, wg_ref,
                       wu_ref, wd_ref, ng_ref, o_ref, a_ref, *, ff_chunk, final_norm):
    x = x_ref[...] + _dot(y_ref[...], wmix_ref[...])
    x = _cross_attention(x, xg_ref, wq_ref, kv_ref, wo_ref, a_ref)
    x = _half_swiglu(x, fg_ref, wg_ref, wu_ref, wd_ref, ff_chunk)
    if final_norm:
        x = _rms(x, ng_ref[...])
    o_ref[...] = x


def _post_mixer(x, y, w_mix, xa_g, w_q, kv, layer, w_o, ffn_g, wg, wu, wd, norm_g, *, final_norm,
                seq, mem_len, tm=TOKEN_TILE, ff_chunk=V7X_MXU_WIDTH):
    t, d = x.shape
    d_ff = wg.shape[2]
    xa_w = XA_HEADS * XA_HEAD_DIM
    assert seq % tm == 0 and d_ff % ff_chunk == 0
    tiles_per_seq = seq // tm
    row = pl.BlockSpec((tm, d), lambda i: (i, 0))
    weight_bytes = (math.prod(w_mix.shape) * w_mix.dtype.itemsize
                    + sum(_layer_bytes(w) for w in (w_q, w_o, wg, wu, wd)))
    return pl.pallas_call(
        functools.partial(_post_mixer_kernel, ff_chunk=ff_chunk, final_norm=final_norm),
        grid=(t // tm,),
        in_specs=[
            row,
            pl.BlockSpec((tm, y.shape[1]), lambda i: (i, 0)),
            _resident(w_mix.shape),
            _resident((1, d), layer),
            _resident((d, xa_w), layer),
            pl.BlockSpec((mem_len, 2 * xa_w), lambda i: (i // tiles_per_seq, layer)),
            _resident((xa_w, d), layer),
            _resident((1, d), layer),
            _resident((d, d_ff), layer),
            _resident((d, d_ff), layer),
            _resident((d_ff, d), layer),
            _resident((1, d)),
        ],
        out_specs=row,
        out_shape=jax.ShapeDtypeStruct((t, d), jnp.float32),
        scratch_shapes=[pltpu.VMEM((tm, xa_w), jnp.bfloat16)],
        compiler_params=_params("parallel", resident_bytes=weight_bytes),
        name="post_mixer",
    )(x, y, w_mix, xa_g, w_q, kv, w_o, ffn_g, wg, wu, wd, norm_g)


def kernel(x, mem, ffn1_norm, ffn1_w_gate, ffn1_w_up, ffn1_w_down, mix_norm, ev_w_in, ev_conv_w,
           ev_sinks, ev_w_out, od_w_in, od_w_out, rel_bias, xa_norm, xa_w_q, xa_w_kv, xa_w_o,
           mem_norm, ffn2_norm, ffn2_w_gate, ffn2_w_up, ffn2_w_down, final_norm):
    b, s, d = x.shape
    depth = ffn1_norm.shape[0]
    mem_len = mem.shape[1]
    bf16 = jnp.bfloat16
    row = lambda g: g.reshape(1, d)

    kv_w = jnp.concatenate([xa_w_kv[l] for l in range(depth)], axis=1).astype(bf16)
    kv = _norm_proj(mem.reshape(b * mem_len, d), row(mem_norm), kv_w,
                    tm=min(TOKEN_TILE, b * mem_len))

    gains = lambda g: g.reshape(depth, 1, d)
    ffn1 = (gains(ffn1_norm), ffn1_w_gate, ffn1_w_up, ffn1_w_down)
    ffn2 = (gains(ffn2_norm), ffn2_w_gate, ffn2_w_up, ffn2_w_down)
    xa_q, xa_o = xa_w_q.astype(bf16), xa_w_o.astype(bf16)

    xt = x.reshape(b * s, d)
    for l in range(depth):
        xt = _ffn(xt, *ffn1, l)
        i = l // 2
        if l % 2 == 0:
            z, vt = _mixer_proj(xt, row(mix_norm[l]), ev_w_in[i].astype(bf16),
                                kv_heads=SWA_KV_HEADS, head_dim=SWA_HEAD_DIM,
                                sum_rows=SWA_SUM_ROWS, seq=s)
            y = _even_mixer(z.reshape(b, s, -1), vt, ev_conv_w[i], ev_sinks[i], rel_bias)
            w_mix = ev_w_out[i]
        else:
            qk, vt = _mixer_proj(xt, row(mix_norm[l]), od_w_in[i].astype(bf16),
                                 kv_heads=MOBA_KV_HEADS, head_dim=MOBA_HEAD_DIM,
                                 sum_rows=MOBA_SUM_ROWS, seq=s)
            y = _moba(qk.reshape(b, s, -1), vt, rel_bias)
            w_mix = od_w_out[i]
        xt = _post_mixer(xt, y.reshape(b * s, -1), w_mix.astype(bf16), gains(xa_norm), xa_q, kv, l,
                         xa_o, *ffn2, row(final_norm), final_norm=(l == depth - 1), seq=s,
                         mem_len=mem_len)
    return xt.reshape(b, s, d)
```
